```python
import math
import jax, jax.numpy as jnp
from jax import lax
import numpy as np


D_MODEL = 1024
BATCH = 16
SEQ = 4096
DEPTH = 2
DEC_BATCH = 2
DEC_SEQ = 8192
PAST_LEN = 128

RMS_EPS = 1e-6
ROPE_THETA = 10000.0
NEG_INF = -1e30

POOL_WIDTH = 512
POOL_WINDOWS = (2, 4, 8, 16)
POOL_GROUPS = len(POOL_WINDOWS)
POOL_GW = POOL_WIDTH // POOL_GROUPS

MLA_HEADS = 8
MLA_NOPE = 64
MLA_ROPE = 32
MLA_V = 64
MLA_QK = MLA_NOPE + MLA_ROPE
MLA_Q_RANK = 384
MLA_KV_RANK = 256
MLA_QBLOCK = 128

DIL_PAIRS = ((128, 1), (512, 4), (2048, 16))
DIL_GROUPS = len(DIL_PAIRS)
DIL_HEADS = 4
DIL_HEAD_DIM = 128
DIL_QKV = DIL_GROUPS * DIL_HEADS * DIL_HEAD_DIM
DIL_BLOCK = 64

N_BRANCH = 3
BRANCH_A = POOL_WIDTH
BRANCH_B = MLA_HEADS * MLA_V
BRANCH_C = DIL_HEADS * DIL_HEAD_DIM
D_FF = 4 * D_MODEL

IN_SPLITS = (POOL_WIDTH, MLA_Q_RANK, MLA_KV_RANK, MLA_ROPE, DIL_QKV, DIL_QKV, DIL_QKV, N_BRANCH * D_MODEL)
IN_WIDTH = sum(IN_SPLITS)
IN_OFFSETS = [int(o) for o in np.cumsum(IN_SPLITS)[:-1]]

kernel_name = "hybrid_pool_mla_dilated_encoder"


def rms_norm(x, g):
    xf = x.astype(jnp.float32)
    y = xf * lax.rsqrt(jnp.mean(xf * xf, axis=-1, keepdims=True) + RMS_EPS)
    return (y * g.astype(jnp.float32)).astype(x.dtype)


def rope(x, pos):
    half = x.shape[-1] // 2
    inv = ROPE_THETA ** (-jnp.arange(half, dtype=jnp.float32) / half)
    ang = pos.astype(jnp.float32)[:, None] * inv[None, :]
    cos = jnp.cos(ang)[:, None, :].astype(x.dtype)
    sin = jnp.sin(ang)[:, None, :].astype(x.dtype)
    x1, x2 = x[..., :half], x[..., half:]
    return jnp.concatenate([x1 * cos - x2 * sin, x2 * cos + x1 * sin], axis=-1)


def pool_mixer(u, w_grp, scale):
    B, S, _ = u.shape
    uf = u.reshape(B, S, POOL_GROUPS, POOL_GW).astype(jnp.float32)
    csum = jnp.concatenate([jnp.zeros((B, 1, POOL_GROUPS, POOL_GW), jnp.float32), jnp.cumsum(uf, axis=1)], axis=1)
    t = jnp.arange(S)
    means = []
    for g, w in enumerate(POOL_WINDOWS):
        lo = jnp.clip(t - w // 2, 0, S)
        hi = jnp.clip(t + w // 2, 0, S)
        cnt = (hi - lo).astype(jnp.float32)
        means.append((csum[:, hi, g] - csum[:, lo, g]) / cnt[None, :, None])
    mixed = (jnp.stack(means, axis=2) - uf).astype(u.dtype)
    y = jnp.einsum('bsgc,gcd->bsgd', mixed, w_grp)
    return y.reshape(B, S, POOL_WIDTH) * scale


def mla_mixer(cq, ckv, kr, q_norm, kv_norm, w_uq, w_uk, w_uv, pos):
    B, S, _ = cq.shape
    q = jnp.einsum('bsr,rn->bsn', rms_norm(cq, q_norm), w_uq).reshape(B, S, MLA_HEADS, MLA_QK)
    q = jnp.concatenate([q[..., :MLA_NOPE], rope(q[..., MLA_NOPE:], pos)], axis=-1)
    c = rms_norm(ckv, kv_norm)
    k_nope = jnp.einsum('bsr,rn->bsn', c, w_uk).reshape(B, S, MLA_HEADS, MLA_NOPE)
    v = jnp.einsum('bsr,rn->bsn', c, w_uv).reshape(B, S, MLA_HEADS, MLA_V)
    k_rope = rope(kr[:, :, None, :], pos)
    k = jnp.concatenate([k_nope, jnp.broadcast_to(k_rope, (B, S, MLA_HEADS, MLA_ROPE))], axis=-1)
    scale = MLA_QK ** -0.5
    nq = S // MLA_QBLOCK
    qb = q.reshape(B, nq, MLA_QBLOCK, MLA_HEADS, MLA_QK).transpose(1, 0, 2, 3, 4)

    def attend(qblk):
        s = jnp.einsum('bqhd,bkhd->bhqk', qblk, k).astype(jnp.float32) * scale
        p = jax.nn.softmax(s, axis=-1).astype(v.dtype)
        return jnp.einsum('bhqk,bkhd->bqhd', p, v)

    o = lax.map(attend, qb)
    return o.transpose(1, 0, 2, 3, 4).reshape(B, S, MLA_HEADS * MLA_V)


def dilated_group(q, k, v, half_w, d):
    B, S, H, dh = q.shape
    L = S // d
    nb = -(-L // DIL_BLOCK)
    Lp = nb * DIL_BLOCK

    def by_residue(x):
        x = x.reshape(B, L, d, H, dh).transpose(0, 2, 1, 3, 4)
        return jnp.pad(x, ((0, 0), (0, 0), (0, Lp - L), (0, 0), (0, 0)))

    def neighbours(x):
        xb = x.reshape(B, d, nb, DIL_BLOCK, H, dh)
        xp = jnp.pad(xb, ((0, 0), (0, 0), (1, 1), (0, 0), (0, 0), (0, 0)))
        return jnp.concatenate([xp[:, :, :-2], xp[:, :, 1:-1], xp[:, :, 2:]], axis=3)

    qb = by_residue(q).reshape(B, d, nb, DIL_BLOCK, H, dh)
    kb = neighbours(by_residue(k))
    vb = neighbours(by_residue(v))
    qpos = jnp.arange(nb)[:, None] * DIL_BLOCK + jnp.arange(DIL_BLOCK)[None, :]
    kpos = (jnp.arange(nb)[:, None] - 1) * DIL_BLOCK + jnp.arange(3 * DIL_BLOCK)[None, :]
    valid = (jnp.abs(qpos[:, :, None] - kpos[:, None, :]) <= half_w) & (kpos[:, None, :] >= 0) & (kpos[:, None, :] < L)
    s = jnp.einsum('brnqhd,brnkhd->brnhqk', qb, kb).astype(jnp.float32) * (dh ** -0.5)
    s = jnp.where(valid[None, None, :, None], s, NEG_INF)
    lse = jax.nn.logsumexp(s, axis=-1)
    p = jnp.exp(s - lse[..., None]).astype(v.dtype)
    o = jnp.einsum('brnhqk,brnkhd->brnqhd', p, vb)
    o = o.reshape(B, d, Lp, H, dh)[:, :, :L].transpose(0, 2, 1, 3, 4).reshape(B, S, H, dh)
    lse = lse.transpose(0, 1, 2, 4, 3).reshape(B, d, Lp, H)[:, :, :L].transpose(0, 2, 1, 3).reshape(B, S, H)
    return o, lse


def dilated_mixer(qd, kd, vd, pos):
    B, S, _ = qd.shape
    shp = (B, S, DIL_GROUPS * DIL_HEADS, DIL_HEAD_DIM)
    q = rope(qd.reshape(shp), pos).reshape(B, S, DIL_GROUPS, DIL_HEADS, DIL_HEAD_DIM)
    k = rope(kd.reshape(shp), pos).reshape(B, S, DIL_GROUPS, DIL_HEADS, DIL_HEAD_DIM)
    v = vd.reshape(B, S, DIL_GROUPS, DIL_HEADS, DIL_HEAD_DIM)
    outs, lses = [], []
    for g, (w, d) in enumerate(DIL_PAIRS):
        o, l = dilated_group(q[:, :, g], k[:, :, g], v[:, :, g], w // (2 * d), d)
        outs.append(o)
        lses.append(l)
    alpha = jax.nn.softmax(jnp.stack(lses, axis=0), axis=0).astype(v.dtype)
    o = jnp.einsum('gbsh,gbshd->bshd', alpha, jnp.stack(outs, axis=0))
    return o.reshape(B, S, BRANCH_C)


def encoder_layer(x, ln1, w_in, pool_w, pool_scale, q_norm, kv_norm, w_uq, w_uk, w_uv,
                  w_a, w_b, w_c, w_o, ln2, w_ff1, w_ff2):
    B, S, D = x.shape
    pos = jnp.arange(S)
    h = rms_norm(x, ln1)
    z = jnp.einsum('bsd,dn->bsn', h, w_in)
    u_pool, cq, ckv, kr, qd, kd, vd, gate_logits = jnp.split(z, IN_OFFSETS, axis=-1)
    a = pool_mixer(u_pool, pool_w, pool_scale)
    b = mla_mixer(cq, ckv, kr, q_norm, kv_norm, w_uq, w_uk, w_uv, pos)
    c = dilated_mixer(qd, kd, vd, pos)
    gates = jax.nn.sigmoid(gate_logits.astype(jnp.float32)).astype(x.dtype).reshape(B, S, N_BRANCH, D)
    merged = (gates[:, :, 0] * jnp.einsum('bsc,cd->bsd', a, w_a)
              + gates[:, :, 1] * jnp.einsum('bsc,cd->bsd', b, w_b)
              + gates[:, :, 2] * jnp.einsum('bsc,cd->bsd', c, w_c))
    x = x + jnp.einsum('bsd,de->bse', merged, w_o)
    h2 = rms_norm(x, ln2)
    ff = jnp.square(jax.nn.relu(jnp.einsum('bsd,df->bsf', h2, w_ff1)))
    return x + jnp.einsum('bsf,fd->bsd', ff, w_ff2)


def setup_inputs(seed: int = 0) -> dict:
    key = jax.random.key(seed)
    ks = jax.random.split(key, 24)

    def w(k, shape, fan_in):
        return jax.random.normal(k, shape, jnp.float32) * (fan_in ** -0.5)

    def gain(k, shape):
        return 1.0 + 0.02 * jax.random.normal(k, shape, jnp.float32)

    return {
        'x_prompt': jax.random.normal(ks[0], (BATCH, SEQ, D_MODEL), jnp.float32),
        'x_sample': jax.random.normal(ks[1], (DEC_BATCH, DEC_SEQ, D_MODEL), jnp.float32),
        'ln1': gain(ks[2], (DEPTH, D_MODEL)),
        'w_in': w(ks[3], (DEPTH, D_MODEL, IN_WIDTH), D_MODEL),
        'pool_w': w(ks[4], (DEPTH, POOL_GROUPS, POOL_GW, POOL_GW), POOL_GW),
        'pool_scale': gain(ks[5], (DEPTH, POOL_WIDTH)),
        'q_norm': gain(ks[6], (DEPTH, MLA_Q_RANK)),
        'kv_norm': gain(ks[7], (DEPTH, MLA_KV_RANK)),
        'w_uq': w(ks[8], (DEPTH, MLA_Q_RANK, MLA_HEADS * MLA_QK), MLA_Q_RANK),
        'w_uk': w(ks[9], (DEPTH, MLA_KV_RANK, MLA_HEADS * MLA_NOPE), MLA_KV_RANK),
        'w_uv': w(ks[10], (DEPTH, MLA_KV_RANK, MLA_HEADS * MLA_V), MLA_KV_RANK),
        'w_a': w(ks[11], (DEPTH, BRANCH_A, D_MODEL), BRANCH_A),
        'w_b': w(ks[12], (DEPTH, BRANCH_B, D_MODEL), BRANCH_B),
        'w_c': w(ks[13], (DEPTH, BRANCH_C, D_MODEL), BRANCH_C),
        'w_o': w(ks[14], (DEPTH, D_MODEL, D_MODEL), D_MODEL),
        'ln2': gain(ks[15], (DEPTH, D_MODEL)),
        'w_ff1': w(ks[16], (DEPTH, D_MODEL, D_FF), D_MODEL),
        'w_ff2': w(ks[17], (DEPTH, D_FF, D_MODEL), D_FF),
        'final_norm': gain(ks[18], (D_MODEL,)),
    }


def reference(x_prompt, x_sample, ln1, w_in, pool_w, pool_scale, q_norm, kv_norm, w_uq, w_uk, w_uv,
              w_a, w_b, w_c, w_o, ln2, w_ff1, w_ff2, final_norm):
    hp, hs = x_prompt, x_sample
    for l in range(DEPTH):
        params = (ln1[l], w_in[l], pool_w[l], pool_scale[l], q_norm[l], kv_norm[l], w_uq[l], w_uk[l], w_uv[l],
                  w_a[l], w_b[l], w_c[l], w_o[l], ln2[l], w_ff1[l], w_ff2[l])
        hp = encoder_layer(hp, *params)
        hs = encoder_layer(hs, *params)
    y_prompt = rms_norm(hp, final_norm)
    y_sample = rms_norm(hs, final_norm)
    return (y_prompt, y_sample)
```

```python
import functools

import numpy as np
import jax
import jax.numpy as jnp
from jax import lax
from jax.experimental import pallas as pl
from jax.experimental.pallas import tpu as pltpu

F32 = jnp.float32
BF16 = jnp.bfloat16

D_MODEL = 1024
RMS_EPS = 1e-6
ROPE_THETA = 10000.0
NEG_INF = -1e30

POOL_WIDTH = 512
POOL_WINDOWS = (2, 4, 8, 16)
POOL_GW = POOL_WIDTH // len(POOL_WINDOWS)
POOL_HALO = 8

MLA_HEADS = 8
MLA_NOPE = 64
MLA_ROPE = 32
MLA_V = 64
MLA_QK = MLA_NOPE + MLA_ROPE
MLA_Q_RANK = 384
MLA_KV_RANK = 256
MLA_HEAD_PAD = 128

DIL_PAIRS = ((128, 1), (512, 4), (2048, 16))
DIL_GROUPS = len(DIL_PAIRS)
DIL_HEADS = 4
DIL_HEAD_DIM = 128
DIL_GROUP_W = DIL_HEADS * DIL_HEAD_DIM
DIL_QKV = DIL_GROUPS * DIL_GROUP_W
DIL_HALF_W = 64
DIL_QBLK = 128

N_BRANCH = 3
BRANCH_W = 512
D_FF = 4 * D_MODEL
FF_CHUNK = 1024

IN_SPLITS = (POOL_WIDTH, MLA_Q_RANK, MLA_KV_RANK, MLA_ROPE, DIL_QKV, DIL_QKV, DIL_QKV, N_BRANCH * D_MODEL)
IN_OFFSETS = [0] + [int(o) for o in np.cumsum(IN_SPLITS)]

VMEM_LIMIT = 56 * 1024 * 1024


def _params(*sem):
    return pltpu.CompilerParams(dimension_semantics=sem, vmem_limit_bytes=VMEM_LIMIT)


def _resident(shape):
    nd = len(shape)
    return pl.BlockSpec(shape, lambda *_: (0,) * nd, pipeline_mode=pl.Buffered(1))


def _rms(xf, g):
    return xf * lax.rsqrt(jnp.mean(xf * xf, axis=-1, keepdims=True) + RMS_EPS) * g


def _dot(a, b):
    return jnp.dot(a, b, preferred_element_type=F32)


def _dot_nt(a, b):
    return lax.dot_general(a, b, (((1,), (1,)), ((), ())), preferred_element_type=F32)


def _dil_proj_kernel(x_ref, g_ref, w_ref, cos_ref, sin_ref, q_ref, k_ref, v_ref):
    h = _rms(x_ref[...], g_ref[...]).astype(BF16)
    cos = cos_ref[...]
    sin = sin_ref[...]
    scale = DIL_HEAD_DIM ** -0.5
    outs = (q_ref, k_ref, v_ref)
    for c in range(3 * DIL_GROUPS):
        which, grp = divmod(c, DIL_GROUPS)
        acc = _dot(h, w_ref[:, c * DIL_GROUP_W:(c + 1) * DIL_GROUP_W])
        for hh in range(DIL_HEADS):
            sl = slice(hh * DIL_HEAD_DIM, (hh + 1) * DIL_HEAD_DIM)
            blk = acc[:, sl]
            if which < 2:
                blk = blk * cos + pltpu.roll(blk, DIL_HEAD_DIM // 2, axis=1) * sin
            if which == 0:
                blk = blk * scale
            outs[which][:, grp * DIL_GROUP_W + sl.start:grp * DIL_GROUP_W + sl.stop] = blk.astype(BF16)


def _dil_proj(x, g, w, cos, sin, S, tm):
    T = x.shape[0]
    nt = S // tm
    row = lambda i: (i, 0)
    tab = lambda i: (i % nt, 0)
    out = jax.ShapeDtypeStruct((T, DIL_QKV), BF16)
    return pl.pallas_call(
        _dil_proj_kernel,
        grid=(T // tm,),
        in_specs=[
            pl.BlockSpec((tm, D_MODEL), row),
            _resident((1, D_MODEL)),
            _resident((D_MODEL, 3 * DIL_QKV)),
            pl.BlockSpec((tm, DIL_HEAD_DIM), tab),
            pl.BlockSpec((tm, DIL_HEAD_DIM), tab),
        ],
        out_specs=[pl.BlockSpec((tm, DIL_QKV), row)] * 3,
        out_shape=[out] * 3,
        compiler_params=_params("parallel"),
        name="dil_proj",
    )(x, g, w, cos, sin)


def _dil_attn_kernel(q_ref, kp_ref, kc_ref, kn_ref, vp_ref, vc_ref, vn_ref, o_ref, lse_ref, kwin, vwin, *, tl, L):
    i = pl.program_id(2)
    hw = DIL_HALF_W
    kwin[0:hw] = kp_ref[0]
    kwin[hw:hw + tl] = kc_ref[0]
    kwin[hw + tl:] = kn_ref[0]
    vwin[0:hw] = vp_ref[0]
    vwin[hw:hw + tl] = vc_ref[0]
    vwin[hw + tl:] = vn_ref[0]
    nk = DIL_QBLK + 2 * hw
    row = lax.broadcasted_iota(jnp.int32, (DIL_QBLK, nk), 0)
    col = lax.broadcasted_iota(jnp.int32, (DIL_QBLK, nk), 1)
    band = (col >= row) & (col - row <= 2 * hw)
    lane = lax.broadcasted_iota(jnp.int32, (DIL_QBLK, 128), 1)
    for j in range(tl // DIL_QBLK):
        r0 = j * DIL_QBLK
        kidx = i * tl + r0 - hw + col
        valid = band & (kidx >= 0) & (kidx < L)
        lse_tile = jnp.zeros((DIL_QBLK, 128), F32)
        for hh in range(DIL_HEADS):
            sl = slice(hh * DIL_HEAD_DIM, (hh + 1) * DIL_HEAD_DIM)
            q = q_ref[0, r0:r0 + DIL_QBLK, sl]
            k = kwin[r0:r0 + nk, sl]
            v = vwin[r0:r0 + nk, sl]
            s = jnp.where(valid, _dot_nt(q, k), NEG_INF)
            m = jnp.max(s, axis=-1, keepdims=True)
            p = jnp.exp(s - m)
            l = jnp.sum(p, axis=-1, keepdims=True)
            o = _dot(p.astype(BF16), v) / l
            o_ref[0, r0:r0 + DIL_QBLK, sl] = o.astype(BF16)
            lse_tile = jnp.where(lane == hh, m + jnp.log(l), lse_tile)
        lse_ref[0, r0:r0 + DIL_QBLK, :] = lse_tile


def _dil_attn(q, k, v, B, S, grp, tl):
    d = DIL_PAIRS[grp][1]
    L = S // d
    T = B * S
    tl = min(tl, L)
    hb = tl // DIL_HALF_W
    nhb = L // DIL_HALF_W
    qv, kv, vv = (a.reshape(B, L, d * DIL_QKV) for a in (q, k, v))
    cur = lambda b, r, i: (b, i, r * DIL_GROUPS + grp)
    prev = lambda b, r, i: (b, jnp.maximum(i * hb - 1, 0), r * DIL_GROUPS + grp)
    nxt = lambda b, r, i: (b, jnp.minimum((i + 1) * hb, nhb - 1), r * DIL_GROUPS + grp)
    outm = lambda b, r, i: (b, i, r)
    blk = lambda n, f: pl.BlockSpec((1, n, DIL_GROUP_W), f)
    o, lse = pl.pallas_call(
        functools.partial(_dil_attn_kernel, tl=tl, L=L),
        grid=(B, d, L // tl),
        in_specs=[blk(tl, cur), blk(DIL_HALF_W, prev), blk(tl, cur), blk(DIL_HALF_W, nxt),
                  blk(DIL_HALF_W, prev), blk(tl, cur), blk(DIL_HALF_W, nxt)],
        out_specs=[pl.BlockSpec((1, tl, DIL_GROUP_W), outm), pl.BlockSpec((1, tl, 128), outm)],
        out_shape=[jax.ShapeDtypeStruct((B, L, d * DIL_GROUP_W), BF16),
                   jax.ShapeDtypeStruct((B, L, d * 128), F32)],
        scratch_shapes=[pltpu.VMEM((tl + 2 * DIL_HALF_W, DIL_GROUP_W), BF16)] * 2,
        compiler_params=_params("parallel", "parallel", "parallel"),
        name=f"dil_attn_g{grp}",
    )(qv, kv, kv, kv, vv, vv, vv)
    return o.reshape(T, DIL_GROUP_W), lse.reshape(T, 128)


def _mla_prep_kernel(x_ref, g_ref, w1_ref, qn_ref, kvn_ref, wq_ref, wqs_ref, wk_ref, wv_ref, cos_ref, sin_ref,
                     u_ref, q_ref, k_ref, v_ref):
    h = _rms(x_ref[...], g_ref[...]).astype(BF16)
    t = _dot(h, w1_ref[...])
    o_cq = POOL_WIDTH
    o_ckv = o_cq + MLA_Q_RANK
    o_kr = o_ckv + MLA_KV_RANK
    u_ref[...] = t[:, :o_cq]
    cqn = _rms(t[:, o_cq:o_ckv], qn_ref[...]).astype(BF16)
    cn = _rms(t[:, o_ckv:o_kr], kvn_ref[...]).astype(BF16)
    cos = cos_ref[...]
    sin = sin_ref[...]
    kr = t[:, o_kr:o_kr + MLA_HEAD_PAD] * cos + t[:, o_kr + MLA_HEAD_PAD:] * sin
    qf = _dot(cqn, wq_ref[...])
    qs = _dot(cqn, wqs_ref[...])
    kf = _dot(cn, wk_ref[...])
    scale = MLA_QK ** -0.5
    for hh in range(MLA_HEADS):
        sl = slice(hh * MLA_HEAD_PAD, (hh + 1) * MLA_HEAD_PAD)
        q_ref[:, sl] = ((qf[:, sl] * cos + qs[:, sl] * sin) * scale).astype(BF16)
        k_ref[:, sl] = (kf[:, sl] + kr).astype(BF16)
    v_ref[...] = _dot(cn, wv_ref[...]).astype(BF16)


def _mla_prep(x, g, w1, qn, kvn, wq, wqs, wk, wv, cos, sin, S, tm):
    T = x.shape[0]
    nt = S // tm
    row = lambda i: (i, 0)
    tab = lambda i: (i % nt, 0)
    hp = MLA_HEADS * MLA_HEAD_PAD
    return pl.pallas_call(
        _mla_prep_kernel,
        grid=(T // tm,),
        in_specs=[
            pl.BlockSpec((tm, D_MODEL), row),
            _resident((1, D_MODEL)),
            _resident(w1.shape),
            _resident((1, MLA_Q_RANK)),
            _resident((1, MLA_KV_RANK)),
            _resident(wq.shape),
            _resident(wqs.shape),
            _resident(wk.shape),
            _resident(wv.shape),
            pl.BlockSpec((tm, MLA_HEAD_PAD), tab),
            pl.BlockSpec((tm, MLA_HEAD_PAD), tab),
        ],
        out_specs=[pl.BlockSpec((tm, POOL_WIDTH), row), pl.BlockSpec((tm, hp), row),
                   pl.BlockSpec((tm, hp), row), pl.BlockSpec((tm, MLA_HEADS * MLA_V), row)],
        out_shape=[jax.ShapeDtypeStruct((T, POOL_WIDTH), F32), jax.ShapeDtypeStruct((T, hp), BF16),
                   jax.ShapeDtypeStruct((T, hp), BF16), jax.ShapeDtypeStruct((T, MLA_HEADS * MLA_V), BF16)],
        compiler_params=_params("parallel"),
        name="mla_prep",
    )(x, g, w1, qn, kvn, wq, wqs, wk, wv, cos, sin)


def _mla_attn_kernel(q_ref, k_ref, v_ref, o_ref):
    v = v_ref[...]
    outs = []
    for hh in range(2):
        sl = slice(hh * MLA_HEAD_PAD, (hh + 1) * MLA_HEAD_PAD)
        s = _dot_nt(q_ref[:, sl], k_ref[:, sl])
        m = jnp.max(s, axis=-1, keepdims=True)
        p = jnp.exp(s - m)
        l = jnp.sum(p, axis=-1, keepdims=True)
        outs.append(_dot(p.astype(BF16), v) / l)
    lane = lax.broadcasted_iota(jnp.int32, outs[0].shape, 1)
    o_ref[...] = jnp.where(lane < MLA_V, outs[0], outs[1]).astype(BF16)


def _mla_attn(q, k, v, B, S, tq):
    T = B * S
    nq = S // tq
    return pl.pallas_call(
        _mla_attn_kernel,
        grid=(B, MLA_HEADS // 2, nq),
        in_specs=[
            pl.BlockSpec((tq, 2 * MLA_HEAD_PAD), lambda b, p, i: (b * nq + i, p)),
            pl.BlockSpec((S, 2 * MLA_HEAD_PAD), lambda b, p, i: (b, p)),
            pl.BlockSpec((S, 2 * MLA_V), lambda b, p, i: (b, p)),
        ],
        out_specs=pl.BlockSpec((tq, 2 * MLA_V), lambda b, p, i: (b * nq + i, p)),
        out_shape=jax.ShapeDtypeStruct((T, MLA_HEADS * MLA_V), BF16),
        compiler_params=_params("parallel", "parallel", "parallel"),
        name="mla_attn",
    )(q, k, v)


def _pool_kernel(up_ref, uc_ref, un_ref, w_ref, sc_ref, a_ref, ext, *, ts, S):
    i = pl.program_id(1)
    n = pl.num_programs(1)
    hl = POOL_HALO
    ext[0:hl] = jnp.where(i > 0, up_ref[...], 0.0)
    ext[hl:hl + ts] = uc_ref[...]
    ext[hl + ts:] = jnp.where(i < n - 1, un_ref[...], 0.0)
    t = i * ts + lax.broadcasted_iota(jnp.int32, (ts, 1), 0)
    for g, w in enumerate(POOL_WINDOWS):
        cols = slice(g * POOL_GW, (g + 1) * POOL_GW)
        acc = ext[hl - w // 2:hl - w // 2 + ts, cols]
        for off in range(-w // 2 + 1, w // 2):
            acc = acc + ext[hl + off:hl + off + ts, cols]
        cnt = (jnp.clip(t + w // 2, 0, S) - jnp.clip(t - w // 2, 0, S)).astype(F32)
        mixed = acc / cnt - uc_ref[:, cols]
        y = _dot(mixed.astype(BF16), w_ref[g])
        a_ref[:, cols] = (y * sc_ref[:, cols]).astype(BF16)


def _pool_mix(u, w, sc, B, S, ts):
    T = B * S
    ns = S // ts
    hb = ts // POOL_HALO
    nhb = T // POOL_HALO
    return pl.pallas_call(
        functools.partial(_pool_kernel, ts=ts, S=S),
        grid=(B, ns),
        in_specs=[
            pl.BlockSpec((POOL_HALO, POOL_WIDTH), lambda b, i: (jnp.maximum((b * ns + i) * hb - 1, 0), 0)),
            pl.BlockSpec((ts, POOL_WIDTH), lambda b, i: (b * ns + i, 0)),
            pl.BlockSpec((POOL_HALO, POOL_WIDTH), lambda b, i: (jnp.minimum((b * ns + i + 1) * hb, nhb - 1), 0)),
            _resident(w.shape),
            _resident((1, POOL_WIDTH)),
        ],
        out_specs=pl.BlockSpec((ts, POOL_WIDTH), lambda b, i: (b * ns + i, 0)),
        out_shape=jax.ShapeDtypeStruct((T, POOL_WIDTH), BF16),
        scratch_shapes=[pltpu.VMEM((ts + 2 * POOL_HALO, POOL_WIDTH), F32)],
        compiler_params=_params("parallel", "parallel"),
        name="pool_mix",
    )(u, u, u, w, sc)


def _merge_kernel(x_ref, g_ref, a_ref, b_ref, o0_ref, o1_ref, o2_ref, l0_ref, l1_ref, l2_ref,
                  wg_ref, wa_ref, wb_ref, wc_ref, wo_ref, x2_ref):
    x = x_ref[...]
    h = _rms(x, g_ref[...]).astype(BF16)
    o_refs = (o0_ref, o1_ref, o2_ref)
    lses = [r[...] for r in (l0_ref, l1_ref, l2_ref)]
    parts = []
    for hh in range(DIL_HEADS):
        sl = slice(hh * DIL_HEAD_DIM, (hh + 1) * DIL_HEAD_DIM)
        ls = [l[:, hh:hh + 1] for l in lses]
        mx = jnp.maximum(jnp.maximum(ls[0], ls[1]), ls[2])
        es = [jnp.exp(l - mx) for l in ls]
        den = es[0] + es[1] + es[2]
        parts.append(sum((e / den) * r[:, sl].astype(F32) for e, r in zip(es, o_refs)))
    c = jnp.concatenate(parts, axis=1).astype(BF16)
    merged = None
    for j, (br, w_ref) in enumerate(((a_ref[...], wa_ref), (b_ref[...], wb_ref), (c, wc_ref))):
        logits = _dot(h, wg_ref[:, j * D_MODEL:(j + 1) * D_MODEL])
        gate = 1.0 / (1.0 + jnp.exp(-logits))
        term = gate * _dot(br, w_ref[...])
        merged = term if merged is None else merged + term
    x2_ref[...] = x + _dot(merged.astype(BF16), wo_ref[...])


def _merge(x, g, a, b, os_, ls_, wg, wa, wb, wc, wo, tm):
    T = x.shape[0]
    row = lambda i: (i, 0)
    br = pl.BlockSpec((tm, BRANCH_W), row)
    ls = pl.BlockSpec((tm, 128), row)
    return pl.pallas_call(
        _merge_kernel,
        grid=(T // tm,),
        in_specs=[pl.BlockSpec((tm, D_MODEL), row), _resident((1, D_MODEL)), br, br, br, br, br, ls, ls, ls,
                  _resident(wg.shape), _resident(wa.shape), _resident(wb.shape), _resident(wc.shape),
                  _resident(wo.shape)],
        out_specs=pl.BlockSpec((tm, D_MODEL), row),
        out_shape=jax.ShapeDtypeStruct((T, D_MODEL), F32),
        compiler_params=_params("parallel"),
        name="merge",
    )(x, g, a, b, *os_, *ls_, wg, wa, wb, wc, wo)


def _ffn_kernel(x_ref, g_ref, w1_ref, w2_ref, gf_ref, y_ref, *, final):
    x = x_ref[...]
    h = _rms(x, g_ref[...]).astype(BF16)
    acc = x
    for c in range(D_FF // FF_CHUNK):
        sl = slice(c * FF_CHUNK, (c + 1) * FF_CHUNK)
        f = jnp.maximum(_dot(h, w1_ref[:, sl]), 0.0)
        acc = acc + _dot((f * f).astype(BF16), w2_ref[sl, :])
    if final:
        acc = _rms(acc, gf_ref[...])
    y_ref[...] = acc


def _ffn(x, g, w1, w2, gf, final, tm):
    T = x.shape[0]
    row = lambda i: (i, 0)
    return pl.pallas_call(
        functools.partial(_ffn_kernel, final=final),
        grid=(T // tm,),
        in_specs=[pl.BlockSpec((tm, D_MODEL), row), _resident((1, D_MODEL)), _resident(w1.shape),
                  _resident(w2.shape), _resident((1, D_MODEL))],
        out_specs=pl.BlockSpec((tm, D_MODEL), row),
        out_shape=jax.ShapeDtypeStruct((T, D_MODEL), F32),
        compiler_params=_params("parallel"),
        name="ffn",
    )(x, g, w1, w2, gf)


def _rope_tables(S):
    pos = jnp.arange(S).astype(F32)[:, None]

    def cs(half):
        inv = ROPE_THETA ** (-jnp.arange(half, dtype=F32) / half)
        ang = pos * inv[None, :]
        return jnp.cos(ang), jnp.sin(ang)

    c, s = cs(DIL_HEAD_DIM // 2)
    dil = (jnp.concatenate([c, c], axis=1), jnp.concatenate([-s, s], axis=1))
    c, s = cs(MLA_ROPE // 2)
    pad = MLA_HEAD_PAD - MLA_QK
    mla = (jnp.concatenate([jnp.ones((S, MLA_NOPE), F32), c, c, jnp.zeros((S, pad), F32)], axis=1),
           jnp.concatenate([jnp.zeros((S, MLA_NOPE), F32), -s, s, jnp.zeros((S, pad), F32)], axis=1))
    return dil, mla


def _layer_weights(l, ln1, w_in, pool_w, pool_scale, q_norm, kv_norm, w_uq, w_uk, w_uv,
                   w_a, w_b, w_c, w_o, ln2, w_ff1, w_ff2):
    o = IN_OFFSETS
    wi = w_in[l]
    half = MLA_ROPE // 2
    pad = MLA_HEAD_PAD - MLA_QK
    kr = wi[:, o[3]:o[4]]
    zl = jnp.zeros((D_MODEL, MLA_NOPE), F32)
    zr = jnp.zeros((D_MODEL, pad), F32)
    kr_placed = jnp.concatenate([zl, kr[:, :half], kr[:, half:], zr], axis=1)
    kr_swapped = jnp.concatenate([zl, kr[:, half:], kr[:, :half], zr], axis=1)
    w1 = jnp.concatenate([wi[:, o[0]:o[3]], kr_placed, kr_swapped], axis=1)
    uq = w_uq[l].reshape(MLA_Q_RANK, MLA_HEADS, MLA_QK)
    zq = jnp.zeros((MLA_Q_RANK, MLA_HEADS, pad), F32)
    wq = jnp.concatenate([uq, zq], axis=2)
    wqs = jnp.concatenate([jnp.zeros((MLA_Q_RANK, MLA_HEADS, MLA_NOPE), F32), uq[:, :, MLA_NOPE + half:],
                           uq[:, :, MLA_NOPE:MLA_NOPE + half], zq], axis=2)
    uk = w_uk[l].reshape(MLA_KV_RANK, MLA_HEADS, MLA_NOPE)
    wk = jnp.concatenate([uk, jnp.zeros((MLA_KV_RANK, MLA_HEADS, MLA_HEAD_PAD - MLA_NOPE), F32)], axis=2)
    hp = MLA_HEADS * MLA_HEAD_PAD
    b16 = lambda a: a.astype(BF16)
    return dict(
        ln1=ln1[l][None], ln2=ln2[l][None],
        w_dil=b16(wi[:, o[4]:o[7]]), w_gate=b16(wi[:, o[7]:o[8]]), w1=b16(w1),
        q_norm=q_norm[l][None], kv_norm=kv_norm[l][None],
        wq=b16(wq.reshape(MLA_Q_RANK, hp)), wqs=b16(wqs.reshape(MLA_Q_RANK, hp)),
        wk=b16(wk.reshape(MLA_KV_RANK, hp)), wv=b16(w_uv[l]),
        pool_w=b16(pool_w[l]), pool_scale=pool_scale[l][None],
        w_a=b16(w_a[l]), w_b=b16(w_b[l]), w_c=b16(w_c[l]), w_o=b16(w_o[l]),
        w_ff1=b16(w_ff1[l]), w_ff2=b16(w_ff2[l]),
    )


def _encoder_layer(x, B, S, p, tabs, gf, final):
    (dcos, dsin), (mcos, msin) = tabs
    tm = min(512, S)
    qd, kd, vd = _dil_proj(x, p["ln1"], p["w_dil"], dcos, dsin, S, tm)
    dil = [_dil_attn(qd, kd, vd, B, S, grp, 256) for grp in range(DIL_GROUPS)]
    u, q, k, v = _mla_prep(x, p["ln1"], p["w1"], p["q_norm"], p["kv_norm"], p["wq"], p["wqs"], p["wk"], p["wv"],
                           mcos, msin, S, tm)
    b = _mla_attn(q, k, v, B, S, 256 if S <= 4096 else 128)
    a = _pool_mix(u, p["pool_w"], p["pool_scale"], B, S, tm)
    x2 = _merge(x, p["ln1"], a, b, [o for o, _ in dil], [l for _, l in dil],
                p["w_gate"], p["w_a"], p["w_b"], p["w_c"], p["w_o"], tm)
    return _ffn(x2, p["ln2"], p["w_ff1"], p["w_ff2"], gf, final, tm)


def kernel(x_prompt, x_sample, ln1, w_in, pool_w, pool_scale, q_norm, kv_norm, w_uq, w_uk, w_uv,
           w_a, w_b, w_c, w_o, ln2, w_ff1, w_ff2, final_norm):
    depth = w_in.shape[0]
    layers = [_layer_weights(l, ln1, w_in, pool_w, pool_scale, q_norm, kv_norm, w_uq, w_uk, w_uv,
                             w_a, w_b, w_c, w_o, ln2, w_ff1, w_ff2) for l in range(depth)]
    gf = final_norm[None]
    outs = []
    for x in (x_prompt, x_sample):
        B, S, D = x.shape
        tabs = _rope_tables(S)
        h = x.reshape(B * S, D)
        for l in range(depth):
            h = _encoder_layer(h, B, S, layers[l], tabs, gf, l == depth - 1)
        outs.append(h.reshape(B, S, D))
    return tuple(outs)
```

```python
import functools

import numpy as np
import jax
import jax.numpy as jnp
from jax import lax
from jax.experimental import pallas as pl
from jax.experimental.pallas import tpu as pltpu

F32 = jnp.float32
BF16 = jnp.bfloat16

D_MODEL = 1024
RMS_EPS = 1e-6
ROPE_THETA = 10000.0
NEG_INF = -1e30

POOL_WIDTH = 512
POOL_WINDOWS = (2, 4, 8, 16)
POOL_GW = POOL_WIDTH // len(POOL_WINDOWS)
POOL_HALO = 8

MLA_HEADS = 8
MLA_NOPE = 64
MLA_ROPE = 32
MLA_V = 64
MLA_QK = MLA_NOPE + MLA_ROPE
MLA_Q_RANK = 384
MLA_KV_RANK = 256
MLA_HEAD_PAD = 128

DIL_PAIRS = ((128, 1), (512, 4), (2048, 16))
DIL_GROUPS = len(DIL_PAIRS)
DIL_HEADS = 4
DIL_HEAD_DIM = 128
DIL_GROUP_W = DIL_HEADS * DIL_HEAD_DIM
DIL_QKV = DIL_GROUPS * DIL_GROUP_W
DIL_HALF_W = 64
DIL_QBLK = 128

N_BRANCH = 3
BRANCH_W = 512
D_FF = 4 * D_MODEL
FF_CHUNK = 1024

IN_SPLITS = (POOL_WIDTH, MLA_Q_RANK, MLA_KV_RANK, MLA_ROPE, DIL_QKV, DIL_QKV, DIL_QKV, N_BRANCH * D_MODEL)
IN_OFFSETS = [0] + [int(o) for o in np.cumsum(IN_SPLITS)]

VMEM_LIMIT = 56 * 1024 * 1024


def _params(*sem):
    return pltpu.CompilerParams(dimension_semantics=sem, vmem_limit_bytes=VMEM_LIMIT)


def _resident(shape):
    nd = len(shape)
    return pl.BlockSpec(shape, lambda *_: (0,) * nd, pipeline_mode=pl.Buffered(1))


def _rms(xf, g):
    return xf * lax.rsqrt(jnp.mean(xf * xf, axis=-1, keepdims=True) + RMS_EPS) * g


def _dot(a, b):
    return jnp.dot(a, b, preferred_element_type=F32)


def _dot_nt(a, b):
    return lax.dot_general(a, b, (((1,), (1,)), ((), ())), preferred_element_type=F32)


def _dil_proj_kernel(x_ref, g_ref, w_ref, cos_ref, sin_ref, *refs, tm):
    out_refs, stage = refs[:-1], refs[-1]
    h = _rms(x_ref[...], g_ref[...]).astype(BF16)
    cos = cos_ref[...]
    sin = sin_ref[...]
    scale = DIL_HEAD_DIM ** -0.5
    for c in range(3 * DIL_GROUPS):
        which, grp = divmod(c, DIL_GROUPS)
        d = DIL_PAIRS[grp][1]
        out = out_refs[c]
        acc = _dot(h, w_ref[:, c * DIL_GROUP_W:(c + 1) * DIL_GROUP_W])
        for hh in range(DIL_HEADS):
            sl = slice(hh * DIL_HEAD_DIM, (hh + 1) * DIL_HEAD_DIM)
            blk = acc[:, sl]
            if which < 2:
                blk = blk * cos + pltpu.roll(blk, DIL_HEAD_DIM // 2, axis=1) * sin
            if which == 0:
                blk = blk * scale
            if d == 1:
                out[0, 0, :, sl] = blk.astype(BF16)
            else:
                stage[hh] = blk
        if d > 1:
            for r in range(d):
                for hh in range(DIL_HEADS):
                    sl = slice(hh * DIL_HEAD_DIM, (hh + 1) * DIL_HEAD_DIM)
                    out[0, r, :, sl] = stage[hh, pl.ds(r, tm // d, stride=d), :].astype(BF16)


def _dil_proj(x, g, w, cos, sin, B, S, tm):
    T = x.shape[0]
    nt = S // tm
    row = lambda i: (i, 0)
    tab = lambda i: (i % nt, 0)
    cls = lambda i: (i // nt, 0, i % nt, 0)
    ds_ = [DIL_PAIRS[c % DIL_GROUPS][1] for c in range(3 * DIL_GROUPS)]
    return pl.pallas_call(
        functools.partial(_dil_proj_kernel, tm=tm),
        grid=(T // tm,),
        in_specs=[
            pl.BlockSpec((tm, D_MODEL), row),
            _resident((1, D_MODEL)),
            _resident((D_MODEL, 3 * DIL_QKV)),
            pl.BlockSpec((tm, DIL_HEAD_DIM), tab),
            pl.BlockSpec((tm, DIL_HEAD_DIM), tab),
        ],
        out_specs=[pl.BlockSpec((1, d, tm // d, DIL_GROUP_W), cls) for d in ds_],
        out_shape=[jax.ShapeDtypeStruct((B, d, S // d, DIL_GROUP_W), BF16) for d in ds_],
        scratch_shapes=[pltpu.VMEM((DIL_HEADS, tm, DIL_HEAD_DIM), F32)],
        compiler_params=_params("parallel"),
        name="dil_proj",
    )(x, g, w, cos, sin)


def _dil_attn_kernel(q_ref, kp_ref, kc_ref, kn_ref, vp_ref, vc_ref, vn_ref, o_ref, lse_ref, kwin, vwin, *, tl, L):
    i = pl.program_id(2)
    hw = DIL_HALF_W
    kwin[0:hw] = kp_ref[0, 0]
    kwin[hw:hw + tl] = kc_ref[0, 0]
    kwin[hw + tl:] = kn_ref[0, 0]
    vwin[0:hw] = vp_ref[0, 0]
    vwin[hw:hw + tl] = vc_ref[0, 0]
    vwin[hw + tl:] = vn_ref[0, 0]
    nk = DIL_QBLK + 2 * hw
    row = lax.broadcasted_iota(jnp.int32, (DIL_QBLK, nk), 0)
    col = lax.broadcasted_iota(jnp.int32, (DIL_QBLK, nk), 1)
    band = (col >= row) & (col - row <= 2 * hw)
    lane = lax.broadcasted_iota(jnp.int32, (DIL_QBLK, 128), 1)
    for j in range(tl // DIL_QBLK):
        r0 = j * DIL_QBLK
        kidx = i * tl + r0 - hw + col
        valid = band & (kidx >= 0) & (kidx < L)
        lse_tile = jnp.zeros((DIL_QBLK, 128), F32)
        for hh in range(DIL_HEADS):
            sl = slice(hh * DIL_HEAD_DIM, (hh + 1) * DIL_HEAD_DIM)
            q = q_ref[0, 0, r0:r0 + DIL_QBLK, sl]
            k = kwin[r0:r0 + nk, sl]
            v = vwin[r0:r0 + nk, sl]
            s = jnp.where(valid, _dot_nt(q, k), NEG_INF)
            m = jnp.max(s, axis=-1, keepdims=True)
            p = jnp.exp(s - m)
            l = jnp.sum(p, axis=-1, keepdims=True)
            o = _dot(p.astype(BF16), v) / l
            o_ref[0, 0, r0:r0 + DIL_QBLK, sl] = o.astype(BF16)
            lse_tile = jnp.where(lane == hh, m + jnp.log(l), lse_tile)
        lse_ref[0, 0, r0:r0 + DIL_QBLK, :] = lse_tile


def _dil_attn(q, k, v, grp, tl):
    B, d, L, _ = q.shape
    tl = min(tl, L)
    hb = tl // DIL_HALF_W
    nhb = L // DIL_HALF_W
    cur = lambda b, r, i: (b, r, i, 0)
    prev = lambda b, r, i: (b, r, jnp.maximum(i * hb - 1, 0), 0)
    nxt = lambda b, r, i: (b, r, jnp.minimum((i + 1) * hb, nhb - 1), 0)
    blk = lambda n, f: pl.BlockSpec((1, 1, n, DIL_GROUP_W), f)
    return pl.pallas_call(
        functools.partial(_dil_attn_kernel, tl=tl, L=L),
        grid=(B, d, L // tl),
        in_specs=[blk(tl, cur), blk(DIL_HALF_W, prev), blk(tl, cur), blk(DIL_HALF_W, nxt),
                  blk(DIL_HALF_W, prev), blk(tl, cur), blk(DIL_HALF_W, nxt)],
        out_specs=[pl.BlockSpec((1, 1, tl, DIL_GROUP_W), cur), pl.BlockSpec((1, 1, tl, 128), cur)],
        out_shape=[jax.ShapeDtypeStruct((B, d, L, DIL_GROUP_W), BF16),
                   jax.ShapeDtypeStruct((B, d, L, 128), F32)],
        scratch_shapes=[pltpu.VMEM((tl + 2 * DIL_HALF_W, DIL_GROUP_W), BF16)] * 2,
        compiler_params=_params("parallel", "parallel", "parallel"),
        name=f"dil_attn_g{grp}",
    )(q, k, k, k, v, v, v)


def _mla_prep_kernel(x_ref, g_ref, w1_ref, qn_ref, kvn_ref, wq_ref, wqs_ref, wk_ref, wv_ref, cos_ref, sin_ref,
                     u_ref, q_ref, k_ref, v_ref):
    h = _rms(x_ref[...], g_ref[...]).astype(BF16)
    t = _dot(h, w1_ref[...])
    o_cq = POOL_WIDTH
    o_ckv = o_cq + MLA_Q_RANK
    o_kr = o_ckv + MLA_KV_RANK
    u_ref[...] = t[:, :o_cq]
    cqn = _rms(t[:, o_cq:o_ckv], qn_ref[...]).astype(BF16)
    cn = _rms(t[:, o_ckv:o_kr], kvn_ref[...]).astype(BF16)
    cos = cos_ref[...]
    sin = sin_ref[...]
    kr = t[:, o_kr:o_kr + MLA_HEAD_PAD] * cos + t[:, o_kr + MLA_HEAD_PAD:] * sin
    qf = _dot(cqn, wq_ref[...])
    qs = _dot(cqn, wqs_ref[...])
    kf = _dot(cn, wk_ref[...])
    scale = MLA_QK ** -0.5
    for hh in range(MLA_HEADS):
        sl = slice(hh * MLA_HEAD_PAD, (hh + 1) * MLA_HEAD_PAD)
        q_ref[:, sl] = ((qf[:, sl] * cos + qs[:, sl] * sin) * scale).astype(BF16)
        k_ref[:, sl] = (kf[:, sl] + kr).astype(BF16)
    v_ref[...] = _dot(cn, wv_ref[...]).astype(BF16)


def _mla_prep(x, g, w1, qn, kvn, wq, wqs, wk, wv, cos, sin, S, tm):
    T = x.shape[0]
    nt = S // tm
    row = lambda i: (i, 0)
    tab = lambda i: (i % nt, 0)
    hp = MLA_HEADS * MLA_HEAD_PAD
    return pl.pallas_call(
        _mla_prep_kernel,
        grid=(T // tm,),
        in_specs=[
            pl.BlockSpec((tm, D_MODEL), row),
            _resident((1, D_MODEL)),
            _resident(w1.shape),
            _resident((1, MLA_Q_RANK)),
            _resident((1, MLA_KV_RANK)),
            _resident(wq.shape),
            _resident(wqs.shape),
            _resident(wk.shape),
            _resident(wv.shape),
            pl.BlockSpec((tm, MLA_HEAD_PAD), tab),
            pl.BlockSpec((tm, MLA_HEAD_PAD), tab),
        ],
        out_specs=[pl.BlockSpec((tm, POOL_WIDTH), row), pl.BlockSpec((tm, hp), row),
                   pl.BlockSpec((tm, hp), row), pl.BlockSpec((tm, MLA_HEADS * MLA_V), row)],
        out_shape=[jax.ShapeDtypeStruct((T, POOL_WIDTH), F32), jax.ShapeDtypeStruct((T, hp), BF16),
                   jax.ShapeDtypeStruct((T, hp), BF16), jax.ShapeDtypeStruct((T, MLA_HEADS * MLA_V), BF16)],
        compiler_params=_params("parallel"),
        name="mla_prep",
    )(x, g, w1, qn, kvn, wq, wqs, wk, wv, cos, sin)


def _mla_attn_kernel(q_ref, k_ref, v_ref, o_ref):
    v = v_ref[...]
    outs = []
    for hh in range(2):
        sl = slice(hh * MLA_HEAD_PAD, (hh + 1) * MLA_HEAD_PAD)
        s = _dot_nt(q_ref[:, sl], k_ref[:, sl])
        m = jnp.max(s, axis=-1, keepdims=True)
        p = jnp.exp(s - m)
        l = jnp.sum(p, axis=-1, keepdims=True)
        outs.append(_dot(p.astype(BF16), v) / l)
    lane = lax.broadcasted_iota(jnp.int32, outs[0].shape, 1)
    o_ref[...] = jnp.where(lane < MLA_V, outs[0], outs[1]).astype(BF16)


def _mla_attn(q, k, v, B, S, tq):
    T = B * S
    nq = S // tq
    return pl.pallas_call(
        _mla_attn_kernel,
        grid=(B, MLA_HEADS // 2, nq),
        in_specs=[
            pl.BlockSpec((tq, 2 * MLA_HEAD_PAD), lambda b, p, i: (b * nq + i, p)),
            pl.BlockSpec((S, 2 * MLA_HEAD_PAD), lambda b, p, i: (b, p)),
            pl.BlockSpec((S, 2 * MLA_V), lambda b, p, i: (b, p)),
        ],
        out_specs=pl.BlockSpec((tq, 2 * MLA_V), lambda b, p, i: (b * nq + i, p)),
        out_shape=jax.ShapeDtypeStruct((T, MLA_HEADS * MLA_V), BF16),
        compiler_params=_params("parallel", "parallel", "parallel"),
        name="mla_attn",
    )(q, k, v)


def _pool_kernel(up_ref, uc_ref, un_ref, w_ref, sc_ref, a_ref, ext, *, ts, S):
    i = pl.program_id(1)
    n = pl.num_programs(1)
    hl = POOL_HALO
    ext[0:hl] = jnp.where(i > 0, up_ref[...], 0.0)
    ext[hl:hl + ts] = uc_ref[...]
    ext[hl + ts:] = jnp.where(i < n - 1, un_ref[...], 0.0)
    t = i * ts + lax.broadcasted_iota(jnp.int32, (ts, 1), 0)
    for g, w in enumerate(POOL_WINDOWS):
        cols = slice(g * POOL_GW, (g + 1) * POOL_GW)
        acc = ext[hl - w // 2:hl - w // 2 + ts, cols]
        for off in range(-w // 2 + 1, w // 2):
            acc = acc + ext[hl + off:hl + off + ts, cols]
        cnt = (jnp.clip(t + w // 2, 0, S) - jnp.clip(t - w // 2, 0, S)).astype(F32)
        mixed = acc / cnt - uc_ref[:, cols]
        y = _dot(mixed.astype(BF16), w_ref[g])
        a_ref[:, cols] = (y * sc_ref[:, cols]).astype(BF16)


def _pool_mix(u, w, sc, B, S, ts):
    T = B * S
    ns = S // ts
    hb = ts // POOL_HALO
    nhb = T // POOL_HALO
    return pl.pallas_call(
        functools.partial(_pool_kernel, ts=ts, S=S),
        grid=(B, ns),
        in_specs=[
            pl.BlockSpec((POOL_HALO, POOL_WIDTH), lambda b, i: (jnp.maximum((b * ns + i) * hb - 1, 0), 0)),
            pl.BlockSpec((ts, POOL_WIDTH), lambda b, i: (b * ns + i, 0)),
            pl.BlockSpec((POOL_HALO, POOL_WIDTH), lambda b, i: (jnp.minimum((b * ns + i + 1) * hb, nhb - 1), 0)),
            _resident(w.shape),
            _resident((1, POOL_WIDTH)),
        ],
        out_specs=pl.BlockSpec((ts, POOL_WIDTH), lambda b, i: (b * ns + i, 0)),
        out_shape=jax.ShapeDtypeStruct((T, POOL_WIDTH), BF16),
        scratch_shapes=[pltpu.VMEM((ts + 2 * POOL_HALO, POOL_WIDTH), F32)],
        compiler_params=_params("parallel", "parallel"),
        name="pool_mix",
    )(u, u, u, w, sc)


def _merge_kernel(x_ref, g_ref, a_ref, b_ref, o0_ref, o1_ref, o2_ref, l0_ref, l1_ref, l2_ref,
                  wg_ref, wa_ref, wb_ref, wc_ref, wo_ref, x2_ref, o_nat, l_nat, *, tm):
    x = x_ref[...]
    h = _rms(x, g_ref[...]).astype(BF16)
    for grp, (o_ref, l_ref) in enumerate(((o0_ref, l0_ref), (o1_ref, l1_ref), (o2_ref, l2_ref))):
        d = DIL_PAIRS[grp][1]
        for r in range(d):
            rows = pl.ds(r, tm // d, stride=d) if d > 1 else slice(None)
            l_nat[grp, rows, :] = l_ref[0, r]
            for hh in range(DIL_HEADS):
                sl = slice(hh * DIL_HEAD_DIM, (hh + 1) * DIL_HEAD_DIM)
                o_nat[grp, hh, rows, :] = o_ref[0, r, :, sl].astype(F32)
    parts = []
    for hh in range(DIL_HEADS):
        ls = [l_nat[grp, :, hh:hh + 1] for grp in range(DIL_GROUPS)]
        mx = jnp.maximum(jnp.maximum(ls[0], ls[1]), ls[2])
        es = [jnp.exp(l - mx) for l in ls]
        den = es[0] + es[1] + es[2]
        parts.append(sum((e / den) * o_nat[grp, hh] for grp, e in enumerate(es)))
    c = jnp.concatenate(parts, axis=1).astype(BF16)
    merged = None
    for j, (br, w_ref) in enumerate(((a_ref[...], wa_ref), (b_ref[...], wb_ref), (c, wc_ref))):
        logits = _dot(h, wg_ref[:, j * D_MODEL:(j + 1) * D_MODEL])
        gate = 1.0 / (1.0 + jnp.exp(-logits))
        term = gate * _dot(br, w_ref[...])
        merged = term if merged is None else merged + term
    x2_ref[...] = x + _dot(merged.astype(BF16), wo_ref[...])


def _merge(x, g, a, b, os_, ls_, wg, wa, wb, wc, wo, S, tm):
    T = x.shape[0]
    nt = S // tm
    row = lambda i: (i, 0)
    cls = lambda i: (i // nt, 0, i % nt, 0)
    br = pl.BlockSpec((tm, BRANCH_W), row)
    ds_ = [d for _, d in DIL_PAIRS]
    return pl.pallas_call(
        functools.partial(_merge_kernel, tm=tm),
        grid=(T // tm,),
        in_specs=[pl.BlockSpec((tm, D_MODEL), row), _resident((1, D_MODEL)), br, br]
        + [pl.BlockSpec((1, d, tm // d, DIL_GROUP_W), cls) for d in ds_]
        + [pl.BlockSpec((1, d, tm // d, 128), cls) for d in ds_]
        + [_resident(wg.shape), _resident(wa.shape), _resident(wb.shape), _resident(wc.shape),
           _resident(wo.shape)],
        out_specs=pl.BlockSpec((tm, D_MODEL), row),
        out_shape=jax.ShapeDtypeStruct((T, D_MODEL), F32),
        scratch_shapes=[pltpu.VMEM((DIL_GROUPS, DIL_HEADS, tm, DIL_HEAD_DIM), F32),
                        pltpu.VMEM((DIL_GROUPS, tm, 128), F32)],
        compiler_params=_params("parallel"),
        name="merge",
    )(x, g, a, b, *os_, *ls_, wg, wa, wb, wc, wo)


def _ffn_kernel(x_ref, g_ref, w1_ref, w2_ref, gf_ref, y_ref, *, final):
    x = x_ref[...]
    h = _rms(x, g_ref[...]).astype(BF16)
    acc = x
    for c in range(D_FF // FF_CHUNK):
        sl = slice(c * FF_CHUNK, (c + 1) * FF_CHUNK)
        f = jnp.maximum(_dot(h, w1_ref[:, sl]), 0.0)
        acc = acc + _dot((f * f).astype(BF16), w2_ref[sl, :])
    if final:
        acc = _rms(acc, gf_ref[...])
    y_ref[...] = acc


def _ffn(x, g, w1, w2, gf, final, tm):
    T = x.shape[0]
    row = lambda i: (i, 0)
    return pl.pallas_call(
        functools.partial(_ffn_kernel, final=final),
        grid=(T // tm,),
        in_specs=[pl.BlockSpec((tm, D_MODEL), row), _resident((1, D_MODEL)), _resident(w1.shape),
                  _resident(w2.shape), _resident((1, D_MODEL))],
        out_specs=pl.BlockSpec((tm, D_MODEL), row),
        out_shape=jax.ShapeDtypeStruct((T, D_MODEL), F32),
        compiler_params=_params("parallel"),
        name="ffn",
    )(x, g, w1, w2, gf)


def _rope_tables(S):
    pos = jnp.arange(S).astype(F32)[:, None]

    def cs(half):
        inv = ROPE_THETA ** (-jnp.arange(half, dtype=F32) / half)
        ang = pos * inv[None, :]
        return jnp.cos(ang), jnp.sin(ang)

    c, s = cs(DIL_HEAD_DIM // 2)
    dil = (jnp.concatenate([c, c], axis=1), jnp.concatenate([-s, s], axis=1))
    c, s = cs(MLA_ROPE // 2)
    pad = MLA_HEAD_PAD - MLA_QK
    mla = (jnp.concatenate([jnp.ones((S, MLA_NOPE), F32), c, c, jnp.zeros((S, pad), F32)], axis=1),
           jnp.concatenate([jnp.zeros((S, MLA_NOPE), F32), -s, s, jnp.zeros((S, pad), F32)], axis=1))
    return dil, mla


def _layer_weights(l, ln1, w_in, pool_w, pool_scale, q_norm, kv_norm, w_uq, w_uk, w_uv,
                   w_a, w_b, w_c, w_o, ln2, w_ff1, w_ff2):
    o = IN_OFFSETS
    wi = w_in[l]
    half = MLA_ROPE // 2
    pad = MLA_HEAD_PAD - MLA_QK
    kr = wi[:, o[3]:o[4]]
    zl = jnp.zeros((D_MODEL, MLA_NOPE), F32)
    zr = jnp.zeros((D_MODEL, pad), F32)
    kr_placed = jnp.concatenate([zl, kr[:, :half], kr[:, half:], zr], axis=1)
    kr_swapped = jnp.concatenate([zl, kr[:, half:], kr[:, :half], zr], axis=1)
    w1 = jnp.concatenate([wi[:, o[0]:o[3]], kr_placed, kr_swapped], axis=1)
    uq = w_uq[l].reshape(MLA_Q_RANK, MLA_HEADS, MLA_QK)
    zq = jnp.zeros((MLA_Q_RANK, MLA_HEADS, pad), F32)
    wq = jnp.concatenate([uq, zq], axis=2)
    wqs = jnp.concatenate([jnp.zeros((MLA_Q_RANK, MLA_HEADS, MLA_NOPE), F32), uq[:, :, MLA_NOPE + half:],
                           uq[:, :, MLA_NOPE:MLA_NOPE + half], zq], axis=2)
    uk = w_uk[l].reshape(MLA_KV_RANK, MLA_HEADS, MLA_NOPE)
    wk = jnp.concatenate([uk, jnp.zeros((MLA_KV_RANK, MLA_HEADS, MLA_HEAD_PAD - MLA_NOPE), F32)], axis=2)
    hp = MLA_HEADS * MLA_HEAD_PAD
    b16 = lambda a: a.astype(BF16)
    return dict(
        ln1=ln1[l][None], ln2=ln2[l][None],
        w_dil=b16(wi[:, o[4]:o[7]]), w_gate=b16(wi[:, o[7]:o[8]]), w1=b16(w1),
        q_norm=q_norm[l][None], kv_norm=kv_norm[l][None],
        wq=b16(wq.reshape(MLA_Q_RANK, hp)), wqs=b16(wqs.reshape(MLA_Q_RANK, hp)),
        wk=b16(wk.reshape(MLA_KV_RANK, hp)), wv=b16(w_uv[l]),
        pool_w=b16(pool_w[l]), pool_scale=pool_scale[l][None],
        w_a=b16(w_a[l]), w_b=b16(w_b[l]), w_c=b16(w_c[l]), w_o=b16(w_o[l]),
        w_ff1=b16(w_ff1[l]), w_ff2=b16(w_ff2[l]),
    )


def _encoder_layer(x, B, S, p, tabs, gf, final):
    (dcos, dsin), (mcos, msin) = tabs
    tm = min(512, S)
    qkv = _dil_proj(x, p["ln1"], p["w_dil"], dcos, dsin, B, S, tm)
    dil = [_dil_attn(qkv[grp], qkv[DIL_GROUPS + grp], qkv[2 * DIL_GROUPS + grp], grp, 256)
           for grp in range(DIL_GROUPS)]
    u, q, k, v = _mla_prep(x, p["ln1"], p["w1"], p["q_norm"], p["kv_norm"], p["wq"], p["wqs"], p["wk"], p["wv"],
                           mcos, msin, S, tm)
    b = _mla_attn(q, k, v, B, S, 256 if S <= 4096 else 128)
    a = _pool_mix(u, p["pool_w"], p["pool_scale"], B, S, tm)
    x2 = _merge(x, p["ln1"], a, b, [o for o, _ in dil], [l for _, l in dil],
                p["w_gate"], p["w_a"], p["w_b"], p["w_c"], p["w_o"], S, tm)
    return _ffn(x2, p["ln2"], p["w_ff1"], p["w_ff2"], gf, final, tm)


def kernel(x_prompt, x_sample, ln1, w_in, pool_w, pool_scale, q_norm, kv_norm, w_uq, w_uk, w_uv,
           w_a, w_b, w_c, w_o, ln2, w_ff1, w_ff2, final_norm):
    depth = w_in.shape[0]
    layers = [_layer_weights(l, ln1, w_in, pool_w, pool_scale, q_norm, kv_norm, w_uq, w_uk, w_uv,
                             w_a, w_b, w_c, w_o, ln2, w_ff1, w_ff2) for l in range(depth)]
    gf = final_norm[None]
    outs = []
    for x in (x_prompt, x_sample):
        B, S, D = x.shape
        tabs = _rope_tables(S)
        h = x.reshape(B * S, D)
        for l in range(depth):
            h = _encoder_layer(h, B, S, layers[l], tabs, gf, l == depth - 1)
        outs.append(h.reshape(B, S, D))
    return tuple(outs)
```

```python
import functools

import numpy as np
import jax
import jax.numpy as jnp
from jax import lax
from jax.experimental import pallas as pl
from jax.experimental.pallas import tpu as pltpu

F32 = jnp.float32
BF16 = jnp.bfloat16

D_MODEL = 1024
RMS_EPS = 1e-6
ROPE_THETA = 10000.0
NEG_INF = -1e30
LOG2_E = 1.4426950408889634

POOL_WIDTH = 512
POOL_WINDOWS = (2, 4, 8, 16)
POOL_GW = POOL_WIDTH // len(POOL_WINDOWS)
POOL_HALO = 8

MLA_HEADS = 8
MLA_NOPE = 64
MLA_ROPE = 32
MLA_V = 64
MLA_QK = MLA_NOPE + MLA_ROPE
MLA_Q_RANK = 384
MLA_KV_RANK = 256
MLA_HEAD_PAD = 128

DIL_PAIRS = ((128, 1), (512, 4), (2048, 16))
DIL_GROUPS = len(DIL_PAIRS)
DIL_HEADS = 4
DIL_HEAD_DIM = 128
DIL_GROUP_W = DIL_HEADS * DIL_HEAD_DIM
DIL_QKV = DIL_GROUPS * DIL_GROUP_W
DIL_HALF_W = 64
DIL_QBLK = 128

N_BRANCH = 3
BRANCH_W = 512
D_FF = 4 * D_MODEL
FF_CHUNK = 1024

IN_SPLITS = (POOL_WIDTH, MLA_Q_RANK, MLA_KV_RANK, MLA_ROPE, DIL_QKV, DIL_QKV, DIL_QKV, N_BRANCH * D_MODEL)
IN_OFFSETS = [0] + [int(o) for o in np.cumsum(IN_SPLITS)]

VMEM_LIMIT = 56 * 1024 * 1024


def _params(*sem):
    return pltpu.CompilerParams(dimension_semantics=sem, vmem_limit_bytes=VMEM_LIMIT)


def _resident(shape):
    nd = len(shape)
    return pl.BlockSpec(shape, lambda *_: (0,) * nd, pipeline_mode=pl.Buffered(1))


def _rms(xf, g):
    return xf * lax.rsqrt(jnp.mean(xf * xf, axis=-1, keepdims=True) + RMS_EPS) * g


def _dot(a, b):
    return jnp.dot(a, b, preferred_element_type=F32)


def _dot_nt(a, b):
    return lax.dot_general(a, b, (((1,), (1,)), ((), ())), preferred_element_type=F32)


def _dil_proj_kernel(x_ref, g_ref, w_ref, cos_ref, sin_ref, *refs, tm):
    out_refs, stage = refs[:-1], refs[-1]
    h = _rms(x_ref[...], g_ref[...]).astype(BF16)
    cos = cos_ref[...]
    sin = sin_ref[...]
    scale = DIL_HEAD_DIM ** -0.5 * LOG2_E
    for c in range(3 * DIL_GROUPS):
        which, grp = divmod(c, DIL_GROUPS)
        d = DIL_PAIRS[grp][1]
        out = out_refs[c]
        acc = _dot(h, w_ref[:, c * DIL_GROUP_W:(c + 1) * DIL_GROUP_W])
        for hh in range(DIL_HEADS):
            sl = slice(hh * DIL_HEAD_DIM, (hh + 1) * DIL_HEAD_DIM)
            blk = acc[:, sl]
            if which < 2:
                blk = blk * cos + pltpu.roll(blk, DIL_HEAD_DIM // 2, axis=1) * sin
            if which == 0:
                blk = blk * scale
            if d == 1:
                out[0, 0, :, sl] = blk.astype(BF16)
            else:
                stage[hh] = blk
        if d > 1:
            for r in range(d):
                for hh in range(DIL_HEADS):
                    sl = slice(hh * DIL_HEAD_DIM, (hh + 1) * DIL_HEAD_DIM)
                    out[0, r, :, sl] = stage[hh, pl.ds(r, tm // d, stride=d), :].astype(BF16)


def _dil_proj(x, g, w, cos, sin, B, S, tm):
    T = x.shape[0]
    nt = S // tm
    row = lambda i: (i, 0)
    tab = lambda i: (i % nt, 0)
    cls = lambda i: (i // nt, 0, i % nt, 0)
    ds_ = [DIL_PAIRS[c % DIL_GROUPS][1] for c in range(3 * DIL_GROUPS)]
    return pl.pallas_call(
        functools.partial(_dil_proj_kernel, tm=tm),
        grid=(T // tm,),
        in_specs=[
            pl.BlockSpec((tm, D_MODEL), row),
            _resident((1, D_MODEL)),
            _resident((D_MODEL, 3 * DIL_QKV)),
            pl.BlockSpec((tm, DIL_HEAD_DIM), tab),
            pl.BlockSpec((tm, DIL_HEAD_DIM), tab),
        ],
        out_specs=[pl.BlockSpec((1, d, tm // d, DIL_GROUP_W), cls) for d in ds_],
        out_shape=[jax.ShapeDtypeStruct((B, d, S // d, DIL_GROUP_W), BF16) for d in ds_],
        scratch_shapes=[pltpu.VMEM((DIL_HEADS, tm, DIL_HEAD_DIM), F32)],
        compiler_params=_params("parallel"),
        name="dil_proj",
    )(x, g, w, cos, sin)


def _dil_attn_kernel(q_ref, kp_ref, kc_ref, kn_ref, vp_ref, vc_ref, vn_ref, o_ref, lse_ref, kwin, vwin,
                     *, tl, L, dr):
    i = pl.program_id(2)
    hw = DIL_HALF_W
    for rr in range(dr):
        kwin[rr, 0:hw] = kp_ref[0, rr]
        kwin[rr, hw:hw + tl] = kc_ref[0, rr]
        kwin[rr, hw + tl:] = kn_ref[0, rr]
        vwin[rr, 0:hw] = vp_ref[0, rr]
        vwin[rr, hw:hw + tl] = vc_ref[0, rr]
        vwin[rr, hw + tl:] = vn_ref[0, rr]
    nk = DIL_QBLK + 2 * hw
    row = lax.broadcasted_iota(jnp.int32, (DIL_QBLK, nk), 0)
    col = lax.broadcasted_iota(jnp.int32, (DIL_QBLK, nk), 1)
    band = (col >= row) & (col - row <= 2 * hw)
    lane = lax.broadcasted_iota(jnp.int32, (DIL_QBLK, 128), 1)
    for j in range(tl // DIL_QBLK):
        r0 = j * DIL_QBLK
        kidx = i * tl + r0 - hw + col
        valid = band & (kidx >= 0) & (kidx < L)
        for rr in range(dr):
            lse_tile = jnp.zeros((DIL_QBLK, 128), F32)
            for hh in range(DIL_HEADS):
                sl = slice(hh * DIL_HEAD_DIM, (hh + 1) * DIL_HEAD_DIM)
                q = q_ref[0, rr, r0:r0 + DIL_QBLK, sl]
                k = kwin[rr, r0:r0 + nk, sl]
                v = vwin[rr, r0:r0 + nk, sl]
                s = jnp.where(valid, _dot_nt(q, k), NEG_INF)
                m = jnp.max(s, axis=-1, keepdims=True)
                p = jnp.exp2(s - m)
                l = jnp.sum(p, axis=-1, keepdims=True)
                o = _dot(p.astype(BF16), v) / l
                o_ref[0, rr, r0:r0 + DIL_QBLK, sl] = o.astype(BF16)
                lse_tile = jnp.where(lane == hh, m + jnp.log2(l), lse_tile)
            lse_ref[0, rr, r0:r0 + DIL_QBLK, :] = lse_tile


def _dil_attn(q, k, v, grp, rows):
    B, d, L, _ = q.shape
    tl = min(rows, L)
    dr = min(d, max(1, rows // tl))
    hb = tl // DIL_HALF_W
    nhb = L // DIL_HALF_W
    cur = lambda b, r, i: (b, r, i, 0)
    prev = lambda b, r, i: (b, r, jnp.maximum(i * hb - 1, 0), 0)
    nxt = lambda b, r, i: (b, r, jnp.minimum((i + 1) * hb, nhb - 1), 0)
    blk = lambda n, f: pl.BlockSpec((1, dr, n, DIL_GROUP_W), f)
    return pl.pallas_call(
        functools.partial(_dil_attn_kernel, tl=tl, L=L, dr=dr),
        grid=(B, d // dr, L // tl),
        in_specs=[blk(tl, cur), blk(DIL_HALF_W, prev), blk(tl, cur), blk(DIL_HALF_W, nxt),
                  blk(DIL_HALF_W, prev), blk(tl, cur), blk(DIL_HALF_W, nxt)],
        out_specs=[pl.BlockSpec((1, dr, tl, DIL_GROUP_W), cur), pl.BlockSpec((1, dr, tl, 128), cur)],
        out_shape=[jax.ShapeDtypeStruct((B, d, L, DIL_GROUP_W), BF16),
                   jax.ShapeDtypeStruct((B, d, L, 128), F32)],
        scratch_shapes=[pltpu.VMEM((dr, tl + 2 * DIL_HALF_W, DIL_GROUP_W), BF16)] * 2,
        compiler_params=_params("parallel", "parallel", "parallel"),
        name=f"dil_attn_g{grp}",
    )(q, k, k, k, v, v, v)


def _mla_prep_kernel(x_ref, g_ref, w1_ref, qn_ref, kvn_ref, wq_ref, wqs_ref, wk_ref, wv_ref, cos_ref, sin_ref,
                     u_ref, q_ref, k_ref, v_ref):
    h = _rms(x_ref[...], g_ref[...]).astype(BF16)
    t = _dot(h, w1_ref[...])
    o_cq = POOL_WIDTH
    o_ckv = o_cq + MLA_Q_RANK
    o_kr = o_ckv + MLA_KV_RANK
    u_ref[...] = t[:, :o_cq]
    cqn = _rms(t[:, o_cq:o_ckv], qn_ref[...]).astype(BF16)
    cn = _rms(t[:, o_ckv:o_kr], kvn_ref[...]).astype(BF16)
    cos = cos_ref[...]
    sin = sin_ref[...]
    kr = t[:, o_kr:o_kr + MLA_HEAD_PAD] * cos + t[:, o_kr + MLA_HEAD_PAD:] * sin
    qf = _dot(cqn, wq_ref[...])
    qs = _dot(cqn, wqs_ref[...])
    kf = _dot(cn, wk_ref[...])
    scale = MLA_QK ** -0.5 * LOG2_E
    for hh in range(MLA_HEADS):
        sl = slice(hh * MLA_HEAD_PAD, (hh + 1) * MLA_HEAD_PAD)
        q_ref[:, sl] = ((qf[:, sl] * cos + qs[:, sl] * sin) * scale).astype(BF16)
        k_ref[:, sl] = (kf[:, sl] + kr).astype(BF16)
    lane = lax.broadcasted_iota(jnp.int32, (1, MLA_HEADS * MLA_HEAD_PAD), 1)
    ones_lane = jnp.where((lane // MLA_HEAD_PAD) % 2 == 0, MLA_V, 0)
    ones = (lane % MLA_HEAD_PAD == ones_lane).astype(F32)
    v_ref[...] = (_dot(cn, wv_ref[...]) + ones).astype(BF16)


def _mla_prep(x, g, w1, qn, kvn, wq, wqs, wk, wv, cos, sin, S, tm):
    T = x.shape[0]
    nt = S // tm
    row = lambda i: (i, 0)
    tab = lambda i: (i % nt, 0)
    hp = MLA_HEADS * MLA_HEAD_PAD
    return pl.pallas_call(
        _mla_prep_kernel,
        grid=(T // tm,),
        in_specs=[
            pl.BlockSpec((tm, D_MODEL), row),
            _resident((1, D_MODEL)),
            _resident(w1.shape),
            _resident((1, MLA_Q_RANK)),
            _resident((1, MLA_KV_RANK)),
            _resident(wq.shape),
            _resident(wqs.shape),
            _resident(wk.shape),
            _resident(wv.shape),
            pl.BlockSpec((tm, MLA_HEAD_PAD), tab),
            pl.BlockSpec((tm, MLA_HEAD_PAD), tab),
        ],
        out_specs=[pl.BlockSpec((tm, POOL_WIDTH), row)] + [pl.BlockSpec((tm, hp), row)] * 3,
        out_shape=[jax.ShapeDtypeStruct((T, POOL_WIDTH), F32)] + [jax.ShapeDtypeStruct((T, hp), BF16)] * 3,
        compiler_params=_params("parallel"),
        name="mla_prep",
    )(x, g, w1, qn, kvn, wq, wqs, wk, wv, cos, sin)


MLA_HEADS_PER_STEP = 4


def _mla_attn_kernel(q_ref, k_ref, v_ref, o_ref):
    lane = lax.broadcasted_iota(jnp.int32, (q_ref.shape[0], 2 * MLA_V), 1)
    for pr in range(MLA_HEADS_PER_STEP // 2):
        outs = []
        for hh in range(2):
            h = 2 * pr + hh
            sl = slice(h * MLA_HEAD_PAD, (h + 1) * MLA_HEAD_PAD)
            s = _dot_nt(q_ref[:, sl], k_ref[:, sl])
            m = jnp.max(s, axis=-1, keepdims=True)
            p = jnp.exp2(s - m).astype(BF16)
            ov = _dot(p, v_ref[:, sl])
            lcol = MLA_V if hh == 0 else 0
            outs.append(ov / ov[:, lcol:lcol + 1])
        o_ref[:, pr * 2 * MLA_V:(pr + 1) * 2 * MLA_V] = jnp.where(lane < MLA_V, outs[0], outs[1]).astype(BF16)


def _mla_attn(q, k, v, B, S, tq):
    T = B * S
    nq = S // tq
    nh = MLA_HEADS_PER_STEP
    kv = pl.BlockSpec((S, nh * MLA_HEAD_PAD), lambda b, p, i: (b, p), pipeline_mode=pl.Buffered(1))
    return pl.pallas_call(
        _mla_attn_kernel,
        grid=(B, MLA_HEADS // nh, nq),
        in_specs=[pl.BlockSpec((tq, nh * MLA_HEAD_PAD), lambda b, p, i: (b * nq + i, p)), kv, kv],
        out_specs=pl.BlockSpec((tq, nh * MLA_V), lambda b, p, i: (b * nq + i, p)),
        out_shape=jax.ShapeDtypeStruct((T, MLA_HEADS * MLA_V), BF16),
        compiler_params=_params("parallel", "parallel", "parallel"),
        name="mla_attn",
    )(q, k, v)


def _pool_kernel(up_ref, uc_ref, un_ref, w_ref, sc_ref, a_ref, ext, *, ts, S):
    i = pl.program_id(1)
    n = pl.num_programs(1)
    hl = POOL_HALO
    ext[0:hl] = jnp.where(i > 0, up_ref[...], 0.0)
    ext[hl:hl + ts] = uc_ref[...]
    ext[hl + ts:] = jnp.where(i < n - 1, un_ref[...], 0.0)
    t = i * ts + lax.broadcasted_iota(jnp.int32, (ts, 1), 0)
    for g, w in enumerate(POOL_WINDOWS):
        cols = slice(g * POOL_GW, (g + 1) * POOL_GW)
        acc = ext[hl - w // 2:hl - w // 2 + ts, cols]
        for off in range(-w // 2 + 1, w // 2):
            acc = acc + ext[hl + off:hl + off + ts, cols]
        cnt = (jnp.clip(t + w // 2, 0, S) - jnp.clip(t - w // 2, 0, S)).astype(F32)
        mixed = acc / cnt - uc_ref[:, cols]
        y = _dot(mixed.astype(BF16), w_ref[g])
        a_ref[:, cols] = (y * sc_ref[:, cols]).astype(BF16)


def _pool_mix(u, w, sc, B, S, ts):
    T = B * S
    ns = S // ts
    hb = ts // POOL_HALO
    nhb = T // POOL_HALO
    return pl.pallas_call(
        functools.partial(_pool_kernel, ts=ts, S=S),
        grid=(B, ns),
        in_specs=[
            pl.BlockSpec((POOL_HALO, POOL_WIDTH), lambda b, i: (jnp.maximum((b * ns + i) * hb - 1, 0), 0)),
            pl.BlockSpec((ts, POOL_WIDTH), lambda b, i: (b * ns + i, 0)),
            pl.BlockSpec((POOL_HALO, POOL_WIDTH), lambda b, i: (jnp.minimum((b * ns + i + 1) * hb, nhb - 1), 0)),
            _resident(w.shape),
            _resident((1, POOL_WIDTH)),
        ],
        out_specs=pl.BlockSpec((ts, POOL_WIDTH), lambda b, i: (b * ns + i, 0)),
        out_shape=jax.ShapeDtypeStruct((T, POOL_WIDTH), BF16),
        scratch_shapes=[pltpu.VMEM((ts + 2 * POOL_HALO, POOL_WIDTH), F32)],
        compiler_params=_params("parallel", "parallel"),
        name="pool_mix",
    )(u, u, u, w, sc)


def _merge_kernel(x_ref, g_ref, a_ref, b_ref, o0_ref, o1_ref, o2_ref, l0_ref, l1_ref, l2_ref,
                  wg_ref, wa_ref, wb_ref, wc_ref, wo_ref, x2_ref, o_nat, l_nat, *, tm):
    x = x_ref[...]
    h = _rms(x, g_ref[...]).astype(BF16)
    for grp, (o_ref, l_ref) in enumerate(((o0_ref, l0_ref), (o1_ref, l1_ref), (o2_ref, l2_ref))):
        d = DIL_PAIRS[grp][1]
        for r in range(d):
            rows = pl.ds(r, tm // d, stride=d) if d > 1 else slice(None)
            l_nat[grp, rows, :] = l_ref[0, r]
            for hh in range(DIL_HEADS):
                sl = slice(hh * DIL_HEAD_DIM, (hh + 1) * DIL_HEAD_DIM)
                o_nat[grp, hh, rows, :] = o_ref[0, r, :, sl].astype(F32)
    parts = []
    for hh in range(DIL_HEADS):
        ls = [l_nat[grp, :, hh:hh + 1] for grp in range(DIL_GROUPS)]
        mx = jnp.maximum(jnp.maximum(ls[0], ls[1]), ls[2])
        es = [jnp.exp2(l - mx) for l in ls]
        den = es[0] + es[1] + es[2]
        parts.append(sum((e / den) * o_nat[grp, hh] for grp, e in enumerate(es)))
    c = jnp.concatenate(parts, axis=1).astype(BF16)
    merged = None
    for j, (br, w_ref) in enumerate(((a_ref[...], wa_ref), (b_ref[...], wb_ref), (c, wc_ref))):
        logits = _dot(h, wg_ref[:, j * D_MODEL:(j + 1) * D_MODEL])
        gate = 1.0 / (1.0 + jnp.exp(-logits))
        term = gate * _dot(br, w_ref[...])
        merged = term if merged is None else merged + term
    x2_ref[...] = x + _dot(merged.astype(BF16), wo_ref[...])


def _merge(x, g, a, b, os_, ls_, wg, wa, wb, wc, wo, S, tm):
    T = x.shape[0]
    nt = S // tm
    row = lambda i: (i, 0)
    cls = lambda i: (i // nt, 0, i % nt, 0)
    br = pl.BlockSpec((tm, BRANCH_W), row)
    ds_ = [d for _, d in DIL_PAIRS]
    return pl.pallas_call(
        functools.partial(_merge_kernel, tm=tm),
        grid=(T // tm,),
        in_specs=[pl.BlockSpec((tm, D_MODEL), row), _resident((1, D_MODEL)), br, br]
        + [pl.BlockSpec((1, d, tm // d, DIL_GROUP_W), cls) for d in ds_]
        + [pl.BlockSpec((1, d, tm // d, 128), cls) for d in ds_]
        + [_resident(wg.shape), _resident(wa.shape), _resident(wb.shape), _resident(wc.shape),
           _resident(wo.shape)],
        out_specs=pl.BlockSpec((tm, D_MODEL), row),
        out_shape=jax.ShapeDtypeStruct((T, D_MODEL), F32),
        scratch_shapes=[pltpu.VMEM((DIL_GROUPS, DIL_HEADS, tm, DIL_HEAD_DIM), F32),
                        pltpu.VMEM((DIL_GROUPS, tm, 128), F32)],
        compiler_params=_params("parallel"),
        name="merge",
    )(x, g, a, b, *os_, *ls_, wg, wa, wb, wc, wo)


def _ffn_kernel(x_ref, g_ref, w1_ref, w2_ref, gf_ref, y_ref, *, final):
    x = x_ref[...]
    h = _rms(x, g_ref[...]).astype(BF16)
    acc = x
    for c in range(D_FF // FF_CHUNK):
        sl = slice(c * FF_CHUNK, (c + 1) * FF_CHUNK)
        f = jnp.maximum(_dot(h, w1_ref[:, sl]), 0.0)
        acc = acc + _dot((f * f).astype(BF16), w2_ref[sl, :])
    if final:
        acc = _rms(acc, gf_ref[...])
    y_ref[...] = acc


def _ffn(x, g, w1, w2, gf, final, tm):
    T = x.shape[0]
    row = lambda i: (i, 0)
    return pl.pallas_call(
        functools.partial(_ffn_kernel, final=final),
        grid=(T // tm,),
        in_specs=[pl.BlockSpec((tm, D_MODEL), row), _resident((1, D_MODEL)), _resident(w1.shape),
                  _resident(w2.shape), _resident((1, D_MODEL))],
        out_specs=pl.BlockSpec((tm, D_MODEL), row),
        out_shape=jax.ShapeDtypeStruct((T, D_MODEL), F32),
        compiler_params=_params("parallel"),
        name="ffn",
    )(x, g, w1, w2, gf)


def _rope_tables(S):
    pos = jnp.arange(S).astype(F32)[:, None]

    def cs(half):
        inv = ROPE_THETA ** (-jnp.arange(half, dtype=F32) / half)
        ang = pos * inv[None, :]
        return jnp.cos(ang), jnp.sin(ang)

    c, s = cs(DIL_HEAD_DIM // 2)
    dil = (jnp.concatenate([c, c], axis=1), jnp.concatenate([-s, s], axis=1))
    c, s = cs(MLA_ROPE // 2)
    pad = MLA_HEAD_PAD - MLA_QK
    mla = (jnp.concatenate([jnp.ones((S, MLA_NOPE), F32), c, c, jnp.zeros((S, pad), F32)], axis=1),
           jnp.concatenate([jnp.zeros((S, MLA_NOPE), F32), -s, s, jnp.zeros((S, pad), F32)], axis=1))
    return dil, mla


def _layer_weights(l, ln1, w_in, pool_w, pool_scale, q_norm, kv_norm, w_uq, w_uk, w_uv,
                   w_a, w_b, w_c, w_o, ln2, w_ff1, w_ff2):
    o = IN_OFFSETS
    wi = w_in[l]
    half = MLA_ROPE // 2
    pad = MLA_HEAD_PAD - MLA_QK
    kr = wi[:, o[3]:o[4]]
    zl = jnp.zeros((D_MODEL, MLA_NOPE), F32)
    zr = jnp.zeros((D_MODEL, pad), F32)
    kr_placed = jnp.concatenate([zl, kr[:, :half], kr[:, half:], zr], axis=1)
    kr_swapped = jnp.concatenate([zl, kr[:, half:], kr[:, :half], zr], axis=1)
    w1 = jnp.concatenate([wi[:, o[0]:o[3]], kr_placed, kr_swapped], axis=1)
    uq = w_uq[l].reshape(MLA_Q_RANK, MLA_HEADS, MLA_QK)
    zq = jnp.zeros((MLA_Q_RANK, MLA_HEADS, pad), F32)
    wq = jnp.concatenate([uq, zq], axis=2)
    wqs = jnp.concatenate([jnp.zeros((MLA_Q_RANK, MLA_HEADS, MLA_NOPE), F32), uq[:, :, MLA_NOPE + half:],
                           uq[:, :, MLA_NOPE:MLA_NOPE + half], zq], axis=2)
    uk = w_uk[l].reshape(MLA_KV_RANK, MLA_HEADS, MLA_NOPE)
    wk = jnp.concatenate([uk, jnp.zeros((MLA_KV_RANK, MLA_HEADS, MLA_HEAD_PAD - MLA_NOPE), F32)], axis=2)
    uv = w_uv[l].reshape(MLA_KV_RANK, MLA_HEADS // 2, 2, MLA_V)
    zv = jnp.zeros((MLA_KV_RANK, MLA_HEADS // 2, MLA_V), F32)
    wv = jnp.stack([jnp.concatenate([uv[:, :, 0], zv], axis=2), jnp.concatenate([zv, uv[:, :, 1]], axis=2)], axis=2)
    hp = MLA_HEADS * MLA_HEAD_PAD
    b16 = lambda a: a.astype(BF16)
    return dict(
        ln1=ln1[l][None], ln2=ln2[l][None],
        w_dil=b16(wi[:, o[4]:o[7]]), w_gate=b16(wi[:, o[7]:o[8]]), w1=b16(w1),
        q_norm=q_norm[l][None], kv_norm=kv_norm[l][None],
        wq=b16(wq.reshape(MLA_Q_RANK, hp)), wqs=b16(wqs.reshape(MLA_Q_RANK, hp)),
        wk=b16(wk.reshape(MLA_KV_RANK, hp)), wv=b16(wv.reshape(MLA_KV_RANK, hp)),
        pool_w=b16(pool_w[l]), pool_scale=pool_scale[l][None],
        w_a=b16(w_a[l]), w_b=b16(w_b[l]), w_c=b16(w_c[l]), w_o=b16(w_o[l]),
        w_ff1=b16(w_ff1[l]), w_ff2=b16(w_ff2[l]),
    )


def _encoder_layer(x, B, S, p, tabs, gf, final):
    (dcos, dsin), (mcos, msin) = tabs
    tm = min(512, S)
    qkv = _dil_proj(x, p["ln1"], p["w_dil"], dcos, dsin, B, S, tm)
    dil = [_dil_attn(qkv[grp], qkv[DIL_GROUPS + grp], qkv[2 * DIL_GROUPS + grp], grp, 1024)
           for grp in range(DIL_GROUPS)]
    u, q, k, v = _mla_prep(x, p["ln1"], p["w1"], p["q_norm"], p["kv_norm"], p["wq"], p["wqs"], p["wk"], p["wv"],
                           mcos, msin, S, tm)
    b = _mla_attn(q, k, v, B, S, min(256, S))
    a = _pool_mix(u, p["pool_w"], p["pool_scale"], B, S, tm)
    x2 = _merge(x, p["ln1"], a, b, [o for o, _ in dil], [l for _, l in dil],
                p["w_gate"], p["w_a"], p["w_b"], p["w_c"], p["w_o"], S, tm)
    return _ffn(x2, p["ln2"], p["w_ff1"], p["w_ff2"], gf, final, tm)


def kernel(x_prompt, x_sample, ln1, w_in, pool_w, pool_scale, q_norm, kv_norm, w_uq, w_uk, w_uv,
           w_a, w_b, w_c, w_o, ln2, w_ff1, w_ff2, final_norm):
    depth = w_in.shape[0]
    layers = [_layer_weights(l, ln1, w_in, pool_w, pool_scale, q_norm, kv_norm, w_uq, w_uk, w_uv,
                             w_a, w_b, w_c, w_o, ln2, w_ff1, w_ff2) for l in range(depth)]
    gf = final_norm[None]
    outs = []
    for x in (x_prompt, x_sample):
        B, S, D = x.shape
        tabs = _rope_tables(S)
        h = x.reshape(B * S, D)
        for l in range(depth):
            h = _encoder_layer(h, B, S, layers[l], tabs, gf, l == depth - 1)
        outs.append(h.reshape(B, S, D))
    return tuple(outs)
```

```python
import functools

import numpy as np
import jax
import jax.numpy as jnp
from jax import lax
from jax.experimental import pallas as pl
from jax.experimental.pallas import tpu as pltpu

F32 = jnp.float32
BF16 = jnp.bfloat16

D_MODEL = 1024
RMS_EPS = 1e-6
ROPE_THETA = 10000.0
NEG_INF = -1e30
LOG2_E = 1.4426950408889634

POOL_WIDTH = 512
POOL_WINDOWS = (2, 4, 8, 16)
POOL_GW = POOL_WIDTH // len(POOL_WINDOWS)
POOL_HALO = 8

MLA_HEADS = 8
MLA_NOPE = 64
MLA_ROPE = 32
MLA_V = 64
MLA_QK = MLA_NOPE + MLA_ROPE
MLA_Q_RANK = 384
MLA_KV_RANK = 256
MLA_HEAD_PAD = 128

DIL_PAIRS = ((128, 1), (512, 4), (2048, 16))
DIL_GROUPS = len(DIL_PAIRS)
DIL_HEADS = 4
DIL_HEAD_DIM = 128
DIL_GROUP_W = DIL_HEADS * DIL_HEAD_DIM
DIL_QKV = DIL_GROUPS * DIL_GROUP_W
DIL_HALF_W = 64
DIL_QBLK = 128

N_BRANCH = 3
BRANCH_W = 512
D_FF = 4 * D_MODEL
FF_CHUNK = 1024

IN_SPLITS = (POOL_WIDTH, MLA_Q_RANK, MLA_KV_RANK, MLA_ROPE, DIL_QKV, DIL_QKV, DIL_QKV, N_BRANCH * D_MODEL)
IN_OFFSETS = [0] + [int(o) for o in np.cumsum(IN_SPLITS)]

VMEM_LIMIT = 56 * 1024 * 1024


def _params(*sem):
    return pltpu.CompilerParams(dimension_semantics=sem, vmem_limit_bytes=VMEM_LIMIT)


def _resident(shape):
    nd = len(shape)
    return pl.BlockSpec(shape, lambda *_: (0,) * nd, pipeline_mode=pl.Buffered(1))


def _rms(xf, g):
    return xf * lax.rsqrt(jnp.mean(xf * xf, axis=-1, keepdims=True) + RMS_EPS) * g


def _dot(a, b):
    return jnp.dot(a, b, preferred_element_type=F32)


def _dot_nt(a, b):
    return lax.dot_general(a, b, (((1,), (1,)), ((), ())), preferred_element_type=F32)


def _mla_proj(h, w1_ref, qn_ref, kvn_ref, wq_ref, wqs_ref, wk_ref, wv_ref, cos_ref, sin_ref,
              u_ref, q_ref, k_ref, v_ref):
    t = _dot(h, w1_ref[...])
    o_cq = POOL_WIDTH
    o_ckv = o_cq + MLA_Q_RANK
    o_kr = o_ckv + MLA_KV_RANK
    u_ref[...] = t[:, :o_cq]
    cqn = _rms(t[:, o_cq:o_ckv], qn_ref[...]).astype(BF16)
    cn = _rms(t[:, o_ckv:o_kr], kvn_ref[...]).astype(BF16)
    cos = cos_ref[...]
    sin = sin_ref[...]
    kr = t[:, o_kr:o_kr + MLA_HEAD_PAD] * cos + t[:, o_kr + MLA_HEAD_PAD:] * sin
    qf = _dot(cqn, wq_ref[...])
    qs = _dot(cqn, wqs_ref[...])
    kf = _dot(cn, wk_ref[...])
    scale = MLA_QK ** -0.5 * LOG2_E
    for hh in range(MLA_HEADS):
        sl = slice(hh * MLA_HEAD_PAD, (hh + 1) * MLA_HEAD_PAD)
        q_ref[:, sl] = ((qf[:, sl] * cos + qs[:, sl] * sin) * scale).astype(BF16)
        k_ref[:, sl] = (kf[:, sl] + kr).astype(BF16)
    lane = lax.broadcasted_iota(jnp.int32, (1, MLA_HEADS * MLA_HEAD_PAD), 1)
    ones_lane = jnp.where((lane // MLA_HEAD_PAD) % 2 == 0, MLA_V, 0)
    ones = (lane % MLA_HEAD_PAD == ones_lane).astype(F32)
    v_ref[...] = (_dot(cn, wv_ref[...]) + ones).astype(BF16)


def _dil_proj(h, w_ref, cos_ref, sin_ref, out_refs, stage, tm):
    cos = cos_ref[...]
    sin = sin_ref[...]
    scale = DIL_HEAD_DIM ** -0.5 * LOG2_E
    n = 0
    for grp in reversed(range(DIL_GROUPS)):
        d = DIL_PAIRS[grp][1]
        for which in range(3):
            c = which * DIL_GROUPS + grp
            out = out_refs[c]
            acc = _dot(h, w_ref[:, c * DIL_GROUP_W:(c + 1) * DIL_GROUP_W])
            for hh in range(DIL_HEADS):
                sl = slice(hh * DIL_HEAD_DIM, (hh + 1) * DIL_HEAD_DIM)
                blk = acc[:, sl]
                if which < 2:
                    blk = blk * cos + pltpu.roll(blk, DIL_HEAD_DIM // 2, axis=1) * sin
                if which == 0:
                    blk = blk * scale
                if d == 1:
                    out[0, 0, :, sl] = blk.astype(BF16)
                else:
                    stage[n % 2, hh] = blk
            if d > 1:
                for r in range(d):
                    for hh in range(DIL_HEADS):
                        sl = slice(hh * DIL_HEAD_DIM, (hh + 1) * DIL_HEAD_DIM)
                        out[0, r, :, sl] = stage[n % 2, hh, pl.ds(r, tm // d, stride=d), :].astype(BF16)
                n += 1


N_DIL_OUT = 3 * DIL_GROUPS


def _in_proj_kernel(x_ref, g_ref, wd_ref, dcos_ref, dsin_ref, w1_ref, qn_ref, kvn_ref, wq_ref, wqs_ref, wk_ref,
                    wv_ref, mcos_ref, msin_ref, *refs, tm):
    dil_out, (u_ref, q_ref, k_ref, v_ref), stage = refs[:N_DIL_OUT], refs[N_DIL_OUT:N_DIL_OUT + 4], refs[-1]
    h = _rms(x_ref[...], g_ref[...]).astype(BF16)
    _mla_proj(h, w1_ref, qn_ref, kvn_ref, wq_ref, wqs_ref, wk_ref, wv_ref, mcos_ref, msin_ref,
              u_ref, q_ref, k_ref, v_ref)
    _dil_proj(h, wd_ref, dcos_ref, dsin_ref, dil_out, stage, tm)


def _in_proj(x, p, tabs, B, S, tm):
    (dcos, dsin), (mcos, msin) = tabs
    T = x.shape[0]
    nt = S // tm
    row = lambda i: (i, 0)
    tab = lambda i: (i % nt, 0)
    cls = lambda i: (i // nt, 0, i % nt, 0)
    hp = MLA_HEADS * MLA_HEAD_PAD
    ds_ = [DIL_PAIRS[c % DIL_GROUPS][1] for c in range(N_DIL_OUT)]
    weights = [p[n] for n in ("w1", "q_norm", "kv_norm", "wq", "wqs", "wk", "wv")]
    return pl.pallas_call(
        functools.partial(_in_proj_kernel, tm=tm),
        grid=(T // tm,),
        in_specs=[pl.BlockSpec((tm, D_MODEL), row), _resident((1, D_MODEL)), _resident(p["w_dil"].shape),
                  pl.BlockSpec((tm, DIL_HEAD_DIM), tab), pl.BlockSpec((tm, DIL_HEAD_DIM), tab)]
        + [_resident(w.shape) for w in weights]
        + [pl.BlockSpec((tm, MLA_HEAD_PAD), tab), pl.BlockSpec((tm, MLA_HEAD_PAD), tab)],
        out_specs=[pl.BlockSpec((1, d, tm // d, DIL_GROUP_W), cls) for d in ds_]
        + [pl.BlockSpec((tm, POOL_WIDTH), row)] + [pl.BlockSpec((tm, hp), row)] * 3,
        out_shape=[jax.ShapeDtypeStruct((B, d, S // d, DIL_GROUP_W), BF16) for d in ds_]
        + [jax.ShapeDtypeStruct((T, POOL_WIDTH), F32)] + [jax.ShapeDtypeStruct((T, hp), BF16)] * 3,
        scratch_shapes=[pltpu.VMEM((2, DIL_HEADS, tm, DIL_HEAD_DIM), F32)],
        compiler_params=_params("parallel"),
        name="in_proj",
    )(x, p["ln1"], p["w_dil"], dcos, dsin, *weights, mcos, msin)


def _dil_attn_kernel(q_ref, kp_ref, kc_ref, kn_ref, vp_ref, vc_ref, vn_ref, o_ref, lse_ref, kwin, vwin,
                     *, tl, L, dr):
    i = pl.program_id(2)
    hw = DIL_HALF_W
    for rr in range(dr):
        kwin[rr, 0:hw] = kp_ref[0, rr]
        kwin[rr, hw:hw + tl] = kc_ref[0, rr]
        kwin[rr, hw + tl:] = kn_ref[0, rr]
        vwin[rr, 0:hw] = vp_ref[0, rr]
        vwin[rr, hw:hw + tl] = vc_ref[0, rr]
        vwin[rr, hw + tl:] = vn_ref[0, rr]
    nk = DIL_QBLK + 2 * hw
    row = lax.broadcasted_iota(jnp.int32, (DIL_QBLK, nk), 0)
    col = lax.broadcasted_iota(jnp.int32, (DIL_QBLK, nk), 1)
    band = (col >= row) & (col - row <= 2 * hw)
    lane = lax.broadcasted_iota(jnp.int32, (DIL_QBLK, 128), 1)
    for j in range(tl // DIL_QBLK):
        r0 = j * DIL_QBLK
        kidx = i * tl + r0 - hw + col
        valid = band & (kidx >= 0) & (kidx < L)
        for rr in range(dr):
            lse_tile = jnp.zeros((DIL_QBLK, 128), F32)
            for hh in range(DIL_HEADS):
                sl = slice(hh * DIL_HEAD_DIM, (hh + 1) * DIL_HEAD_DIM)
                q = q_ref[0, rr, r0:r0 + DIL_QBLK, sl]
                k = kwin[rr, r0:r0 + nk, sl]
                v = vwin[rr, r0:r0 + nk, sl]
                s = jnp.where(valid, _dot_nt(q, k), NEG_INF)
                m = jnp.max(s, axis=-1, keepdims=True)
                p = jnp.exp2(s - m)
                l = jnp.sum(p, axis=-1, keepdims=True)
                o = _dot(p.astype(BF16), v) / l
                o_ref[0, rr, r0:r0 + DIL_QBLK, sl] = o.astype(BF16)
                lse_tile = jnp.where(lane == hh, m + jnp.log2(l), lse_tile)
            lse_ref[0, rr, r0:r0 + DIL_QBLK, :] = lse_tile


def _dil_attn(q, k, v, grp, rows):
    B, d, L, _ = q.shape
    tl = min(rows, L)
    dr = min(d, max(1, rows // tl))
    hb = tl // DIL_HALF_W
    nhb = L // DIL_HALF_W
    cur = lambda b, r, i: (b, r, i, 0)
    prev = lambda b, r, i: (b, r, jnp.maximum(i * hb - 1, 0), 0)
    nxt = lambda b, r, i: (b, r, jnp.minimum((i + 1) * hb, nhb - 1), 0)
    blk = lambda n, f: pl.BlockSpec((1, dr, n, DIL_GROUP_W), f)
    return pl.pallas_call(
        functools.partial(_dil_attn_kernel, tl=tl, L=L, dr=dr),
        grid=(B, d // dr, L // tl),
        in_specs=[blk(tl, cur), blk(DIL_HALF_W, prev), blk(tl, cur), blk(DIL_HALF_W, nxt),
                  blk(DIL_HALF_W, prev), blk(tl, cur), blk(DIL_HALF_W, nxt)],
        out_specs=[pl.BlockSpec((1, dr, tl, DIL_GROUP_W), cur), pl.BlockSpec((1, dr, tl, 128), cur)],
        out_shape=[jax.ShapeDtypeStruct((B, d, L, DIL_GROUP_W), BF16),
                   jax.ShapeDtypeStruct((B, d, L, 128), F32)],
        scratch_shapes=[pltpu.VMEM((dr, tl + 2 * DIL_HALF_W, DIL_GROUP_W), BF16)] * 2,
        compiler_params=_params("parallel", "parallel", "parallel"),
        name=f"dil_attn_g{grp}",
    )(q, k, k, k, v, v, v)


MLA_HEADS_PER_STEP = 4


def _mla_attn_kernel(q_ref, k_ref, v_ref, o_ref):
    lane = lax.broadcasted_iota(jnp.int32, (q_ref.shape[0], 2 * MLA_V), 1)
    for pr in range(MLA_HEADS_PER_STEP // 2):
        outs = []
        for hh in range(2):
            h = 2 * pr + hh
            sl = slice(h * MLA_HEAD_PAD, (h + 1) * MLA_HEAD_PAD)
            s = _dot_nt(q_ref[:, sl], k_ref[:, sl])
            m = jnp.max(s, axis=-1, keepdims=True)
            p = jnp.exp2(s - m).astype(BF16)
            ov = _dot(p, v_ref[:, sl])
            lcol = MLA_V if hh == 0 else 0
            outs.append(ov / ov[:, lcol:lcol + 1])
        o_ref[:, pr * 2 * MLA_V:(pr + 1) * 2 * MLA_V] = jnp.where(lane < MLA_V, outs[0], outs[1]).astype(BF16)


def _mla_attn(q, k, v, B, S, tq):
    T = B * S
    nq = S // tq
    nh = MLA_HEADS_PER_STEP
    kv = pl.BlockSpec((S, nh * MLA_HEAD_PAD), lambda b, p, i: (b, p), pipeline_mode=pl.Buffered(1))
    return pl.pallas_call(
        _mla_attn_kernel,
        grid=(B, MLA_HEADS // nh, nq),
        in_specs=[pl.BlockSpec((tq, nh * MLA_HEAD_PAD), lambda b, p, i: (b * nq + i, p)), kv, kv],
        out_specs=pl.BlockSpec((tq, nh * MLA_V), lambda b, p, i: (b * nq + i, p)),
        out_shape=jax.ShapeDtypeStruct((T, MLA_HEADS * MLA_V), BF16),
        compiler_params=_params("parallel", "parallel", "parallel"),
        name="mla_attn",
    )(q, k, v)


def _pool_mix(up_ref, uc_ref, un_ref, w_ref, sc_ref, ext, iseq, nt, tm, S):
    hl = POOL_HALO
    ext[0:hl] = jnp.where(iseq > 0, up_ref[...], 0.0)
    ext[hl:hl + tm] = uc_ref[...]
    ext[hl + tm:] = jnp.where(iseq < nt - 1, un_ref[...], 0.0)
    t = iseq * tm + lax.broadcasted_iota(jnp.int32, (tm, 1), 0)
    parts = []
    for g, w in enumerate(POOL_WINDOWS):
        cols = slice(g * POOL_GW, (g + 1) * POOL_GW)
        acc = ext[hl - w // 2:hl - w // 2 + tm, cols]
        for off in range(-w // 2 + 1, w // 2):
            acc = acc + ext[hl + off:hl + off + tm, cols]
        cnt = (jnp.clip(t + w // 2, 0, S) - jnp.clip(t - w // 2, 0, S)).astype(F32)
        mixed = acc / cnt - uc_ref[:, cols]
        parts.append(_dot(mixed.astype(BF16), w_ref[g]) * sc_ref[:, cols])
    return jnp.concatenate(parts, axis=1)


def _dil_combine(o_refs, l_refs, o_nat, l_nat, tm):
    for grp, (o_ref, l_ref) in enumerate(zip(o_refs, l_refs)):
        d = DIL_PAIRS[grp][1]
        for r in range(d):
            rows = pl.ds(r, tm // d, stride=d) if d > 1 else slice(None)
            l_nat[grp, rows, :] = l_ref[0, r]
            for hh in range(DIL_HEADS):
                sl = slice(hh * DIL_HEAD_DIM, (hh + 1) * DIL_HEAD_DIM)
                o_nat[grp, hh, rows, :] = o_ref[0, r, :, sl].astype(F32)
    parts = []
    for hh in range(DIL_HEADS):
        ls = [l_nat[grp, :, hh:hh + 1] for grp in range(DIL_GROUPS)]
        mx = jnp.maximum(jnp.maximum(ls[0], ls[1]), ls[2])
        es = [jnp.exp2(l - mx) for l in ls]
        den = es[0] + es[1] + es[2]
        parts.append(sum((e / den) * o_nat[grp, hh] for grp, e in enumerate(es)))
    return jnp.concatenate(parts, axis=1)


def _merge_kernel(x_ref, g_ref, up_ref, uc_ref, un_ref, b_ref, o0_ref, o1_ref, o2_ref, l0_ref, l1_ref, l2_ref,
                  pw_ref, ps_ref, wg_ref, wa_ref, wb_ref, wc_ref, wo_ref, x2_ref, ext, o_nat, l_nat, *, tm, nt, S):
    x = x_ref[...]
    h = _rms(x, g_ref[...]).astype(BF16)
    iseq = pl.program_id(0) % nt
    a = _pool_mix(up_ref, uc_ref, un_ref, pw_ref, ps_ref, ext, iseq, nt, tm, S).astype(BF16)
    c = _dil_combine((o0_ref, o1_ref, o2_ref), (l0_ref, l1_ref, l2_ref), o_nat, l_nat, tm).astype(BF16)
    merged = None
    for j, (br, w_ref) in enumerate(((a, wa_ref), (b_ref[...], wb_ref), (c, wc_ref))):
        logits = _dot(h, wg_ref[:, j * D_MODEL:(j + 1) * D_MODEL])
        gate = 1.0 / (1.0 + jnp.exp(-logits))
        term = gate * _dot(br, w_ref[...])
        merged = term if merged is None else merged + term
    x2_ref[...] = x + _dot(merged.astype(BF16), wo_ref[...])


def _merge(x, u, b, os_, ls_, p, S, tm):
    T = x.shape[0]
    nt = S // tm
    hb = tm // POOL_HALO
    nhb = T // POOL_HALO
    row = lambda i: (i, 0)
    cls = lambda i: (i // nt, 0, i % nt, 0)
    br = pl.BlockSpec((tm, BRANCH_W), row)
    halo = lambda f: pl.BlockSpec((POOL_HALO, POOL_WIDTH), f)
    ds_ = [d for _, d in DIL_PAIRS]
    weights = [p[n] for n in ("pool_w", "pool_scale", "w_gate", "w_a", "w_b", "w_c", "w_o")]
    return pl.pallas_call(
        functools.partial(_merge_kernel, tm=tm, nt=nt, S=S),
        grid=(T // tm,),
        in_specs=[pl.BlockSpec((tm, D_MODEL), row), _resident((1, D_MODEL)),
                  halo(lambda i: (jnp.maximum(i * hb - 1, 0), 0)), br,
                  halo(lambda i: (jnp.minimum((i + 1) * hb, nhb - 1), 0)), br]
        + [pl.BlockSpec((1, d, tm // d, DIL_GROUP_W), cls) for d in ds_]
        + [pl.BlockSpec((1, d, tm // d, 128), cls) for d in ds_]
        + [_resident(w.shape) for w in weights],
        out_specs=pl.BlockSpec((tm, D_MODEL), row),
        out_shape=jax.ShapeDtypeStruct((T, D_MODEL), F32),
        scratch_shapes=[pltpu.VMEM((tm + 2 * POOL_HALO, POOL_WIDTH), F32),
                        pltpu.VMEM((DIL_GROUPS, DIL_HEADS, tm, DIL_HEAD_DIM), F32),
                        pltpu.VMEM((DIL_GROUPS, tm, 128), F32)],
        compiler_params=_params("parallel"),
        name="merge",
    )(x, p["ln1"], u, u, u, b, *os_, *ls_, *weights)


def _ffn_kernel(x_ref, g_ref, w1_ref, w2_ref, gf_ref, y_ref, *, final):
    x = x_ref[...]
    h = _rms(x, g_ref[...]).astype(BF16)
    acc = x
    for c in range(D_FF // FF_CHUNK):
        sl = slice(c * FF_CHUNK, (c + 1) * FF_CHUNK)
        f = jnp.maximum(_dot(h, w1_ref[:, sl]), 0.0)
        acc = acc + _dot((f * f).astype(BF16), w2_ref[sl, :])
    if final:
        acc = _rms(acc, gf_ref[...])
    y_ref[...] = acc


def _ffn(x, g, w1, w2, gf, final, tm):
    T = x.shape[0]
    row = lambda i: (i, 0)
    return pl.pallas_call(
        functools.partial(_ffn_kernel, final=final),
        grid=(T // tm,),
        in_specs=[pl.BlockSpec((tm, D_MODEL), row), _resident((1, D_MODEL)), _resident(w1.shape),
                  _resident(w2.shape), _resident((1, D_MODEL))],
        out_specs=pl.BlockSpec((tm, D_MODEL), row),
        out_shape=jax.ShapeDtypeStruct((T, D_MODEL), F32),
        compiler_params=_params("parallel"),
        name="ffn",
    )(x, g, w1, w2, gf)


def _rope_tables(S):
    pos = jnp.arange(S).astype(F32)[:, None]

    def cs(half):
        inv = ROPE_THETA ** (-jnp.arange(half, dtype=F32) / half)
        ang = pos * inv[None, :]
        return jnp.cos(ang), jnp.sin(ang)

    c, s = cs(DIL_HEAD_DIM // 2)
    dil = (jnp.concatenate([c, c], axis=1), jnp.concatenate([-s, s], axis=1))
    c, s = cs(MLA_ROPE // 2)
    pad = MLA_HEAD_PAD - MLA_QK
    mla = (jnp.concatenate([jnp.ones((S, MLA_NOPE), F32), c, c, jnp.zeros((S, pad), F32)], axis=1),
           jnp.concatenate([jnp.zeros((S, MLA_NOPE), F32), -s, s, jnp.zeros((S, pad), F32)], axis=1))
    return dil, mla


def _layer_weights(l, ln1, w_in, pool_w, pool_scale, q_norm, kv_norm, w_uq, w_uk, w_uv,
                   w_a, w_b, w_c, w_o, ln2, w_ff1, w_ff2):
    o = IN_OFFSETS
    wi = w_in[l]
    half = MLA_ROPE // 2
    pad = MLA_HEAD_PAD - MLA_QK
    kr = wi[:, o[3]:o[4]]
    zl = jnp.zeros((D_MODEL, MLA_NOPE), F32)
    zr = jnp.zeros((D_MODEL, pad), F32)
    kr_placed = jnp.concatenate([zl, kr[:, :half], kr[:, half:], zr], axis=1)
    kr_swapped = jnp.concatenate([zl, kr[:, half:], kr[:, :half], zr], axis=1)
    w1 = jnp.concatenate([wi[:, o[0]:o[3]], kr_placed, kr_swapped], axis=1)
    uq = w_uq[l].reshape(MLA_Q_RANK, MLA_HEADS, MLA_QK)
    zq = jnp.zeros((MLA_Q_RANK, MLA_HEADS, pad), F32)
    wq = jnp.concatenate([uq, zq], axis=2)
    wqs = jnp.concatenate([jnp.zeros((MLA_Q_RANK, MLA_HEADS, MLA_NOPE), F32), uq[:, :, MLA_NOPE + half:],
                           uq[:, :, MLA_NOPE:MLA_NOPE + half], zq], axis=2)
    uk = w_uk[l].reshape(MLA_KV_RANK, MLA_HEADS, MLA_NOPE)
    wk = jnp.concatenate([uk, jnp.zeros((MLA_KV_RANK, MLA_HEADS, MLA_HEAD_PAD - MLA_NOPE), F32)], axis=2)
    uv = w_uv[l].reshape(MLA_KV_RANK, MLA_HEADS // 2, 2, MLA_V)
    zv = jnp.zeros((MLA_KV_RANK, MLA_HEADS // 2, MLA_V), F32)
    wv = jnp.stack([jnp.concatenate([uv[:, :, 0], zv], axis=2), jnp.concatenate([zv, uv[:, :, 1]], axis=2)], axis=2)
    hp = MLA_HEADS * MLA_HEAD_PAD
    b16 = lambda a: a.astype(BF16)
    return dict(
        ln1=ln1[l][None], ln2=ln2[l][None],
        w_dil=b16(wi[:, o[4]:o[7]]), w_gate=b16(wi[:, o[7]:o[8]]), w1=b16(w1),
        q_norm=q_norm[l][None], kv_norm=kv_norm[l][None],
        wq=b16(wq.reshape(MLA_Q_RANK, hp)), wqs=b16(wqs.reshape(MLA_Q_RANK, hp)),
        wk=b16(wk.reshape(MLA_KV_RANK, hp)), wv=b16(wv.reshape(MLA_KV_RANK, hp)),
        pool_w=b16(pool_w[l]), pool_scale=pool_scale[l][None],
        w_a=b16(w_a[l]), w_b=b16(w_b[l]), w_c=b16(w_c[l]), w_o=b16(w_o[l]),
        w_ff1=b16(w_ff1[l]), w_ff2=b16(w_ff2[l]),
    )


def _encoder_layer(x, B, S, p, tabs, gf, final):
    tm = min(512, S)
    outs = _in_proj(x, p, tabs, B, S, tm)
    qkv, (u, q, k, v) = outs[:N_DIL_OUT], outs[N_DIL_OUT:]
    dil = [_dil_attn(qkv[grp], qkv[DIL_GROUPS + grp], qkv[2 * DIL_GROUPS + grp], grp, 1024)
           for grp in range(DIL_GROUPS)]
    b = _mla_attn(q, k, v, B, S, min(256, S))
    x2 = _merge(x, u, b, [o for o, _ in dil], [l for _, l in dil], p, S, tm)
    return _ffn(x2, p["ln2"], p["w_ff1"], p["w_ff2"], gf, final, tm)


def kernel(x_prompt, x_sample, ln1, w_in, pool_w, pool_scale, q_norm, kv_norm, w_uq, w_uk, w_uv,
           w_a, w_b, w_c, w_o, ln2, w_ff1, w_ff2, final_norm):
    depth = w_in.shape[0]
    layers = [_layer_weights(l, ln1, w_in, pool_w, pool_scale, q_norm, kv_norm, w_uq, w_uk, w_uv,
                             w_a, w_b, w_c, w_o, ln2, w_ff1, w_ff2) for l in range(depth)]
    gf = final_norm[None]
    outs = []
    for x in (x_prompt, x_sample):
        B, S, D = x.shape
        tabs = _rope_tables(S)
        h = x.reshape(B * S, D)
        for l in range(depth):
            h = _encoder_layer(h, B, S, layers[l], tabs, gf, l == depth - 1)
        outs.append(h.reshape(B, S, D))
    return tuple(outs)
```

```python
import functools

import numpy as np
import jax
import jax.numpy as jnp
from jax import lax
from jax.experimental import pallas as pl
from jax.experimental.pallas import tpu as pltpu

F32 = jnp.float32
BF16 = jnp.bfloat16

D_MODEL = 1024
RMS_EPS = 1e-6
ROPE_THETA = 10000.0
NEG_INF = -1e30
LOG2_E = 1.4426950408889634

POOL_WIDTH = 512
POOL_WINDOWS = (2, 4, 8, 16)
POOL_GW = POOL_WIDTH // len(POOL_WINDOWS)
POOL_HALO = 8

MLA_HEADS = 8
MLA_NOPE = 64
MLA_ROPE = 32
MLA_V = 64
MLA_QK = MLA_NOPE + MLA_ROPE
MLA_Q_RANK = 384
MLA_KV_RANK = 256
MLA_HEAD_PAD = 128

DIL_PAIRS = ((128, 1), (512, 4), (2048, 16))
DIL_GROUPS = len(DIL_PAIRS)
DIL_HEADS = 4
DIL_HEAD_DIM = 128
DIL_GROUP_W = DIL_HEADS * DIL_HEAD_DIM
DIL_QKV = DIL_GROUPS * DIL_GROUP_W
DIL_HALF_W = 64
DIL_QBLK = 128
DIL_SPLIT = 4

N_BRANCH = 3
BRANCH_W = 512
D_FF = 4 * D_MODEL
FF_CHUNK = 1024

IN_SPLITS = (POOL_WIDTH, MLA_Q_RANK, MLA_KV_RANK, MLA_ROPE, DIL_QKV, DIL_QKV, DIL_QKV, N_BRANCH * D_MODEL)
IN_OFFSETS = [0] + [int(o) for o in np.cumsum(IN_SPLITS)]

VMEM_LIMIT = 56 * 1024 * 1024


def _params(*sem):
    return pltpu.CompilerParams(dimension_semantics=sem, vmem_limit_bytes=VMEM_LIMIT)


def _resident(shape):
    nd = len(shape)
    return pl.BlockSpec(shape, lambda *_: (0,) * nd, pipeline_mode=pl.Buffered(1))


def _rms(xf, g):
    return xf * lax.rsqrt(jnp.mean(xf * xf, axis=-1, keepdims=True) + RMS_EPS) * g


def _dot(a, b):
    return jnp.dot(a, b, preferred_element_type=F32)


def _dot_nt(a, b):
    return lax.dot_general(a, b, (((1,), (1,)), ((), ())), preferred_element_type=F32)


def _mla_proj(h, w1_ref, qn_ref, kvn_ref, wq_ref, wqs_ref, wk_ref, wv_ref, cos_ref, sin_ref,
              u_ref, q_ref, k_ref, v_ref):
    t = _dot(h, w1_ref[...])
    o_cq = POOL_WIDTH
    o_ckv = o_cq + MLA_Q_RANK
    o_kr = o_ckv + MLA_KV_RANK
    u_ref[...] = t[:, :o_cq]
    cqn = _rms(t[:, o_cq:o_ckv], qn_ref[...]).astype(BF16)
    cn = _rms(t[:, o_ckv:o_kr], kvn_ref[...]).astype(BF16)
    cos = cos_ref[...]
    sin = sin_ref[...]
    kr = t[:, o_kr:o_kr + MLA_HEAD_PAD] * cos + t[:, o_kr + MLA_HEAD_PAD:] * sin
    qf = _dot(cqn, wq_ref[...])
    qs = _dot(cqn, wqs_ref[...])
    kf = _dot(cn, wk_ref[...])
    scale = MLA_QK ** -0.5 * LOG2_E
    for hh in range(MLA_HEADS):
        sl = slice(hh * MLA_HEAD_PAD, (hh + 1) * MLA_HEAD_PAD)
        q_ref[:, sl] = ((qf[:, sl] * cos + qs[:, sl] * sin) * scale).astype(BF16)
        k_ref[:, sl] = (kf[:, sl] + kr).astype(BF16)
    lane = lax.broadcasted_iota(jnp.int32, (1, MLA_HEADS * MLA_HEAD_PAD), 1)
    ones_lane = jnp.where((lane // MLA_HEAD_PAD) % 2 == 0, MLA_V, 0)
    ones = (lane % MLA_HEAD_PAD == ones_lane).astype(F32)
    v_ref[...] = (_dot(cn, wv_ref[...]) + ones).astype(BF16)


def _dil_proj(h, w_ref, cos_ref, sin_ref, out_refs, stage, mid, tm):
    cos = cos_ref[...]
    sin = sin_ref[...]
    scale = DIL_HEAD_DIM ** -0.5 * LOG2_E
    n = 0
    for grp in reversed(range(DIL_GROUPS)):
        d = DIL_PAIRS[grp][1]
        for which in range(3):
            c = which * DIL_GROUPS + grp
            out = out_refs[c]
            acc = _dot(h, w_ref[:, c * DIL_GROUP_W:(c + 1) * DIL_GROUP_W])
            for hh in range(DIL_HEADS):
                sl = slice(hh * DIL_HEAD_DIM, (hh + 1) * DIL_HEAD_DIM)
                blk = acc[:, sl]
                if which < 2:
                    blk = blk * cos + pltpu.roll(blk, DIL_HEAD_DIM // 2, axis=1) * sin
                if which == 0:
                    blk = blk * scale
                if d == 1:
                    out[0, 0, :, sl] = blk.astype(BF16)
                else:
                    stage[n % 2, hh] = blk
            if d > 1:
                for hh in range(DIL_HEADS):
                    sl = slice(hh * DIL_HEAD_DIM, (hh + 1) * DIL_HEAD_DIM)
                    if d == DIL_SPLIT * DIL_SPLIT:
                        for lo in range(DIL_SPLIT):
                            mid[n % 2, hh, lo] = stage[n % 2, hh, pl.ds(lo, tm // DIL_SPLIT, stride=DIL_SPLIT), :]
                        for r in range(d):
                            hi, lo = divmod(r, DIL_SPLIT)
                            rows = pl.ds(hi, tm // d, stride=DIL_SPLIT)
                            out[0, r, :, sl] = mid[n % 2, hh, lo, rows, :].astype(BF16)
                    else:
                        for r in range(d):
                            out[0, r, :, sl] = stage[n % 2, hh, pl.ds(r, tm // d, stride=d), :].astype(BF16)
                n += 1


N_DIL_OUT = 3 * DIL_GROUPS


def _in_proj_kernel(x_ref, g_ref, wd_ref, dcos_ref, dsin_ref, w1_ref, qn_ref, kvn_ref, wq_ref, wqs_ref, wk_ref,
                    wv_ref, mcos_ref, msin_ref, *refs, tm):
    dil_out, (u_ref, q_ref, k_ref, v_ref), (stage, mid) = refs[:N_DIL_OUT], refs[N_DIL_OUT:N_DIL_OUT + 4], refs[-2:]
    h = _rms(x_ref[...], g_ref[...]).astype(BF16)
    _mla_proj(h, w1_ref, qn_ref, kvn_ref, wq_ref, wqs_ref, wk_ref, wv_ref, mcos_ref, msin_ref,
              u_ref, q_ref, k_ref, v_ref)
    _dil_proj(h, wd_ref, dcos_ref, dsin_ref, dil_out, stage, mid, tm)


def _in_proj(x, p, tabs, B, S, tm):
    (dcos, dsin), (mcos, msin) = tabs
    T = x.shape[0]
    nt = S // tm
    row = lambda i: (i, 0)
    tab = lambda i: (i % nt, 0)
    cls = lambda i: (i // nt, 0, i % nt, 0)
    hp = MLA_HEADS * MLA_HEAD_PAD
    ds_ = [DIL_PAIRS[c % DIL_GROUPS][1] for c in range(N_DIL_OUT)]
    weights = [p[n] for n in ("w1", "q_norm", "kv_norm", "wq", "wqs", "wk", "wv")]
    return pl.pallas_call(
        functools.partial(_in_proj_kernel, tm=tm),
        grid=(T // tm,),
        in_specs=[pl.BlockSpec((tm, D_MODEL), row), _resident((1, D_MODEL)), _resident(p["w_dil"].shape),
                  pl.BlockSpec((tm, DIL_HEAD_DIM), tab), pl.BlockSpec((tm, DIL_HEAD_DIM), tab)]
        + [_resident(w.shape) for w in weights]
        + [pl.BlockSpec((tm, MLA_HEAD_PAD), tab), pl.BlockSpec((tm, MLA_HEAD_PAD), tab)],
        out_specs=[pl.BlockSpec((1, d, tm // d, DIL_GROUP_W), cls) for d in ds_]
        + [pl.BlockSpec((tm, POOL_WIDTH), row)] + [pl.BlockSpec((tm, hp), row)] * 3,
        out_shape=[jax.ShapeDtypeStruct((B, d, S // d, DIL_GROUP_W), BF16) for d in ds_]
        + [jax.ShapeDtypeStruct((T, POOL_WIDTH), F32)] + [jax.ShapeDtypeStruct((T, hp), BF16)] * 3,
        scratch_shapes=[pltpu.VMEM((2, DIL_HEADS, tm, DIL_HEAD_DIM), F32),
                        pltpu.VMEM((2, DIL_HEADS, DIL_SPLIT, tm // DIL_SPLIT, DIL_HEAD_DIM), F32)],
        compiler_params=_params("parallel"),
        name="in_proj",
    )(x, p["ln1"], p["w_dil"], dcos, dsin, *weights, mcos, msin)


def _dil_attn_kernel(q_ref, kp_ref, kc_ref, kn_ref, vp_ref, vc_ref, vn_ref, o_ref, lse_ref, kwin, vwin,
                     *, tl, L, dr):
    i = pl.program_id(2)
    hw = DIL_HALF_W
    for rr in range(dr):
        kwin[rr, 0:hw] = kp_ref[0, rr]
        kwin[rr, hw:hw + tl] = kc_ref[0, rr]
        kwin[rr, hw + tl:] = kn_ref[0, rr]
        vwin[rr, 0:hw] = vp_ref[0, rr]
        vwin[rr, hw:hw + tl] = vc_ref[0, rr]
        vwin[rr, hw + tl:] = vn_ref[0, rr]
    nk = DIL_QBLK + 2 * hw
    row = lax.broadcasted_iota(jnp.int32, (DIL_QBLK, nk), 0)
    col = lax.broadcasted_iota(jnp.int32, (DIL_QBLK, nk), 1)
    band = (col >= row) & (col - row <= 2 * hw)
    lane = lax.broadcasted_iota(jnp.int32, (DIL_QBLK, 128), 1)
    for j in range(tl // DIL_QBLK):
        r0 = j * DIL_QBLK
        kidx = i * tl + r0 - hw + col
        valid = band & (kidx >= 0) & (kidx < L)
        for rr in range(dr):
            lse_tile = jnp.zeros((DIL_QBLK, 128), F32)
            for hh in range(DIL_HEADS):
                sl = slice(hh * DIL_HEAD_DIM, (hh + 1) * DIL_HEAD_DIM)
                q = q_ref[0, rr, r0:r0 + DIL_QBLK, sl]
                k = kwin[rr, r0:r0 + nk, sl]
                v = vwin[rr, r0:r0 + nk, sl]
                s = jnp.where(valid, _dot_nt(q, k), NEG_INF)
                m = jnp.max(s, axis=-1, keepdims=True)
                p = jnp.exp2(s - m)
                l = jnp.sum(p, axis=-1, keepdims=True)
                o = _dot(p.astype(BF16), v) / l
                o_ref[0, rr, r0:r0 + DIL_QBLK, sl] = o.astype(BF16)
                lse_tile = jnp.where(lane == hh, m + jnp.log2(l), lse_tile)
            lse_ref[0, rr, r0:r0 + DIL_QBLK, :] = lse_tile


def _dil_attn(q, k, v, grp, rows):
    B, d, L, _ = q.shape
    tl = min(rows, L)
    dr = min(d, max(1, rows // tl))
    hb = tl // DIL_HALF_W
    nhb = L // DIL_HALF_W
    cur = lambda b, r, i: (b, r, i, 0)
    prev = lambda b, r, i: (b, r, jnp.maximum(i * hb - 1, 0), 0)
    nxt = lambda b, r, i: (b, r, jnp.minimum((i + 1) * hb, nhb - 1), 0)
    blk = lambda n, f: pl.BlockSpec((1, dr, n, DIL_GROUP_W), f)
    return pl.pallas_call(
        functools.partial(_dil_attn_kernel, tl=tl, L=L, dr=dr),
        grid=(B, d // dr, L // tl),
        in_specs=[blk(tl, cur), blk(DIL_HALF_W, prev), blk(tl, cur), blk(DIL_HALF_W, nxt),
                  blk(DIL_HALF_W, prev), blk(tl, cur), blk(DIL_HALF_W, nxt)],
        out_specs=[pl.BlockSpec((1, dr, tl, DIL_GROUP_W), cur), pl.BlockSpec((1, dr, tl, 128), cur)],
        out_shape=[jax.ShapeDtypeStruct((B, d, L, DIL_GROUP_W), BF16),
                   jax.ShapeDtypeStruct((B, d, L, 128), F32)],
        scratch_shapes=[pltpu.VMEM((dr, tl + 2 * DIL_HALF_W, DIL_GROUP_W), BF16)] * 2,
        compiler_params=_params("parallel", "parallel", "parallel"),
        name=f"dil_attn_g{grp}",
    )(q, k, k, k, v, v, v)


MLA_HEADS_PER_STEP = 4


def _mla_attn_kernel(q_ref, k_ref, v_ref, o_ref):
    lane = lax.broadcasted_iota(jnp.int32, (q_ref.shape[0], 2 * MLA_V), 1)
    head = lambda h: slice(h * MLA_HEAD_PAD, (h + 1) * MLA_HEAD_PAD)
    scores = lambda h: _dot_nt(q_ref[:, head(h)], k_ref[:, head(h)])
    outs = []
    s_next = scores(0)
    for h in range(MLA_HEADS_PER_STEP):
        s = s_next
        if h + 1 < MLA_HEADS_PER_STEP:
            s_next = scores(h + 1)
        m = jnp.max(s, axis=-1, keepdims=True)
        p = jnp.exp2(s - m).astype(BF16)
        ov = _dot(p, v_ref[:, head(h)])
        lcol = MLA_V if h % 2 == 0 else 0
        outs.append(ov / ov[:, lcol:lcol + 1])
    for pr in range(MLA_HEADS_PER_STEP // 2):
        pair = jnp.where(lane < MLA_V, outs[2 * pr], outs[2 * pr + 1])
        o_ref[:, pr * 2 * MLA_V:(pr + 1) * 2 * MLA_V] = pair.astype(BF16)


def _mla_attn(q, k, v, B, S, tq):
    T = B * S
    nq = S // tq
    nh = MLA_HEADS_PER_STEP
    kv = pl.BlockSpec((S, nh * MLA_HEAD_PAD), lambda b, p, i: (b, p), pipeline_mode=pl.Buffered(1))
    return pl.pallas_call(
        _mla_attn_kernel,
        grid=(B, MLA_HEADS // nh, nq),
        in_specs=[pl.BlockSpec((tq, nh * MLA_HEAD_PAD), lambda b, p, i: (b * nq + i, p)), kv, kv],
        out_specs=pl.BlockSpec((tq, nh * MLA_V), lambda b, p, i: (b * nq + i, p)),
        out_shape=jax.ShapeDtypeStruct((T, MLA_HEADS * MLA_V), BF16),
        compiler_params=_params("parallel", "parallel", "parallel"),
        name="mla_attn",
    )(q, k, v)


def _pool_fill(up_ref, uc_ref, un_ref, ext, iseq, nt, tm):
    hl = POOL_HALO
    ext[0:hl] = jnp.where(iseq > 0, up_ref[...], 0.0)
    ext[hl:hl + tm] = uc_ref[...]
    ext[hl + tm:] = jnp.where(iseq < nt - 1, un_ref[...], 0.0)


def _pool_group(g, uc_ref, w_ref, sc_ref, ext, iseq, tm, S):
    hl, w = POOL_HALO, POOL_WINDOWS[g]
    cols = slice(g * POOL_GW, (g + 1) * POOL_GW)
    t = iseq * tm + lax.broadcasted_iota(jnp.int32, (tm, 1), 0)
    acc = ext[hl - w // 2:hl - w // 2 + tm, cols]
    for off in range(-w // 2 + 1, w // 2):
        acc = acc + ext[hl + off:hl + off + tm, cols]
    cnt = (jnp.clip(t + w // 2, 0, S) - jnp.clip(t - w // 2, 0, S)).astype(F32)
    mixed = acc / cnt - uc_ref[:, cols]
    return _dot(mixed.astype(BF16), w_ref[g]) * sc_ref[:, cols]


def _dil_to_token_order(grp, o_ref, l_ref, o_nat, l_nat, tm):
    d = DIL_PAIRS[grp][1]
    for r in range(d):
        rows = pl.ds(r, tm // d, stride=d) if d > 1 else slice(None)
        l_nat[grp, rows, :] = l_ref[0, r]
        for hh in range(DIL_HEADS):
            sl = slice(hh * DIL_HEAD_DIM, (hh + 1) * DIL_HEAD_DIM)
            o_nat[grp, hh, rows, :] = o_ref[0, r, :, sl].astype(F32)


def _dil_combine_head(hh, o_nat, l_nat):
    ls = [l_nat[grp, :, hh:hh + 1] for grp in range(DIL_GROUPS)]
    mx = jnp.maximum(jnp.maximum(ls[0], ls[1]), ls[2])
    es = [jnp.exp2(l - mx) for l in ls]
    den = es[0] + es[1] + es[2]
    return sum((e / den) * o_nat[grp, hh] for grp, e in enumerate(es))


def _gate2(z_half):
    return jnp.tanh(z_half) + 1.0


def _merge_kernel(x_ref, g_ref, up_ref, uc_ref, un_ref, b_ref, o0_ref, o1_ref, o2_ref, l0_ref, l1_ref, l2_ref,
                  pw_ref, ps_ref, wg_ref, wa_ref, wb_ref, wc_ref, wo_ref, x2_ref, ext, o_nat, l_nat, *, tm, nt, S):
    x = x_ref[...]
    h = _rms(x, g_ref[...]).astype(BF16)
    iseq = pl.program_id(0) % nt
    hw = D_MODEL // 2
    cols = lambda hf: slice(hf * hw, (hf + 1) * hw)
    logits = lambda j, hf: _dot(h, wg_ref[:, j * D_MODEL + hf * hw:j * D_MODEL + (hf + 1) * hw])
    pool = lambda g: _pool_group(g, uc_ref, pw_ref, ps_ref, ext, iseq, tm, S)
    o_refs, l_refs = (o0_ref, o1_ref, o2_ref), (l0_ref, l1_ref, l2_ref)
    b = b_ref[...]

    l1a = logits(1, 0)
    _pool_fill(up_ref, uc_ref, un_ref, ext, iseq, nt, tm)
    a0 = pool(0)
    tba = _dot(b, wb_ref[:, cols(0)])
    a1 = pool(1)
    l1b = logits(1, 1)
    a2 = pool(2)
    tbb = _dot(b, wb_ref[:, cols(1)])
    l0a = logits(0, 0)
    a3 = pool(3)
    a = jnp.concatenate([a0, a1, a2, a3], axis=1).astype(BF16)
    l0b = logits(0, 1)
    _dil_to_token_order(0, o_refs[0], l_refs[0], o_nat, l_nat, tm)
    _dil_to_token_order(1, o_refs[1], l_refs[1], o_nat, l_nat, tm)
    taa = _dot(a, wa_ref[:, cols(0)])
    _dil_to_token_order(2, o_refs[2], l_refs[2], o_nat, l_nat, tm)
    l2a = logits(2, 0)
    c01 = [_dil_combine_head(hh, o_nat, l_nat) for hh in (0, 1)]
    tab = _dot(a, wa_ref[:, cols(1)])
    c23 = [_dil_combine_head(hh, o_nat, l_nat) for hh in (2, 3)]
    c = jnp.concatenate(c01 + c23, axis=1).astype(BF16)
    l2b = logits(2, 1)
    gba = _gate2(l1a) * tba + _gate2(l0a) * taa
    tca = _dot(c, wc_ref[:, cols(0)])
    gbb = _gate2(l1b) * tbb + _gate2(l0b) * tab
    tcb = _dot(c, wc_ref[:, cols(1)])
    ma = (gba + _gate2(l2a) * tca).astype(BF16)
    acc = x + _dot(ma, wo_ref[cols(0), :])
    mb = (gbb + _gate2(l2b) * tcb).astype(BF16)
    x2_ref[...] = acc + _dot(mb, wo_ref[cols(1), :])


def _merge(x, u, b, os_, ls_, p, S, tm):
    T = x.shape[0]
    nt = S // tm
    hb = tm // POOL_HALO
    nhb = T // POOL_HALO
    row = lambda i: (i, 0)
    cls = lambda i: (i // nt, 0, i % nt, 0)
    br = pl.BlockSpec((tm, BRANCH_W), row)
    halo = lambda f: pl.BlockSpec((POOL_HALO, POOL_WIDTH), f)
    ds_ = [d for _, d in DIL_PAIRS]
    weights = [p[n] for n in ("pool_w", "pool_scale", "w_gate", "w_a", "w_b", "w_c", "w_o")]
    return pl.pallas_call(
        functools.partial(_merge_kernel, tm=tm, nt=nt, S=S),
        grid=(T // tm,),
        in_specs=[pl.BlockSpec((tm, D_MODEL), row), _resident((1, D_MODEL)),
                  halo(lambda i: (jnp.maximum(i * hb - 1, 0), 0)), br,
                  halo(lambda i: (jnp.minimum((i + 1) * hb, nhb - 1), 0)), br]
        + [pl.BlockSpec((1, d, tm // d, DIL_GROUP_W), cls) for d in ds_]
        + [pl.BlockSpec((1, d, tm // d, 128), cls) for d in ds_]
        + [_resident(w.shape) for w in weights],
        out_specs=pl.BlockSpec((tm, D_MODEL), row),
        out_shape=jax.ShapeDtypeStruct((T, D_MODEL), F32),
        scratch_shapes=[pltpu.VMEM((tm + 2 * POOL_HALO, POOL_WIDTH), F32),
                        pltpu.VMEM((DIL_GROUPS, DIL_HEADS, tm, DIL_HEAD_DIM), F32),
                        pltpu.VMEM((DIL_GROUPS, tm, 128), F32)],
        compiler_params=_params("parallel"),
        name="merge",
    )(x, p["ln1"], u, u, u, b, *os_, *ls_, *weights)


def _ffn_kernel(x_ref, g_ref, w1_ref, w2_ref, gf_ref, y_ref, *, final):
    x = x_ref[...]
    h = _rms(x, g_ref[...]).astype(BF16)
    acc = x
    for c in range(D_FF // FF_CHUNK):
        sl = slice(c * FF_CHUNK, (c + 1) * FF_CHUNK)
        f = jnp.maximum(_dot(h, w1_ref[:, sl]), 0.0)
        acc = acc + _dot((f * f).astype(BF16), w2_ref[sl, :])
    if final:
        acc = _rms(acc, gf_ref[...])
    y_ref[...] = acc


def _ffn(x, g, w1, w2, gf, final, tm):
    T = x.shape[0]
    row = lambda i: (i, 0)
    return pl.pallas_call(
        functools.partial(_ffn_kernel, final=final),
        grid=(T // tm,),
        in_specs=[pl.BlockSpec((tm, D_MODEL), row), _resident((1, D_MODEL)), _resident(w1.shape),
                  _resident(w2.shape), _resident((1, D_MODEL))],
        out_specs=pl.BlockSpec((tm, D_MODEL), row),
        out_shape=jax.ShapeDtypeStruct((T, D_MODEL), F32),
        compiler_params=_params("parallel"),
        name="ffn",
    )(x, g, w1, w2, gf)


def _rope_tables(S):
    pos = jnp.arange(S).astype(F32)[:, None]

    def cs(half):
        inv = ROPE_THETA ** (-jnp.arange(half, dtype=F32) / half)
        ang = pos * inv[None, :]
        return jnp.cos(ang), jnp.sin(ang)

    c, s = cs(DIL_HEAD_DIM // 2)
    dil = (jnp.concatenate([c, c], axis=1), jnp.concatenate([-s, s], axis=1))
    c, s = cs(MLA_ROPE // 2)
    pad = MLA_HEAD_PAD - MLA_QK
    mla = (jnp.concatenate([jnp.ones((S, MLA_NOPE), F32), c, c, jnp.zeros((S, pad), F32)], axis=1),
           jnp.concatenate([jnp.zeros((S, MLA_NOPE), F32), -s, s, jnp.zeros((S, pad), F32)], axis=1))
    return dil, mla


def _layer_weights(l, ln1, w_in, pool_w, pool_scale, q_norm, kv_norm, w_uq, w_uk, w_uv,
                   w_a, w_b, w_c, w_o, ln2, w_ff1, w_ff2):
    o = IN_OFFSETS
    wi = w_in[l]
    half = MLA_ROPE // 2
    pad = MLA_HEAD_PAD - MLA_QK
    kr = wi[:, o[3]:o[4]]
    zl = jnp.zeros((D_MODEL, MLA_NOPE), F32)
    zr = jnp.zeros((D_MODEL, pad), F32)
    kr_placed = jnp.concatenate([zl, kr[:, :half], kr[:, half:], zr], axis=1)
    kr_swapped = jnp.concatenate([zl, kr[:, half:], kr[:, :half], zr], axis=1)
    w1 = jnp.concatenate([wi[:, o[0]:o[3]], kr_placed, kr_swapped], axis=1)
    uq = w_uq[l].reshape(MLA_Q_RANK, MLA_HEADS, MLA_QK)
    zq = jnp.zeros((MLA_Q_RANK, MLA_HEADS, pad), F32)
    wq = jnp.concatenate([uq, zq], axis=2)
    wqs = jnp.concatenate([jnp.zeros((MLA_Q_RANK, MLA_HEADS, MLA_NOPE), F32), uq[:, :, MLA_NOPE + half:],
                           uq[:, :, MLA_NOPE:MLA_NOPE + half], zq], axis=2)
    uk = w_uk[l].reshape(MLA_KV_RANK, MLA_HEADS, MLA_NOPE)
    wk = jnp.concatenate([uk, jnp.zeros((MLA_KV_RANK, MLA_HEADS, MLA_HEAD_PAD - MLA_NOPE), F32)], axis=2)
    uv = w_uv[l].reshape(MLA_KV_RANK, MLA_HEADS // 2, 2, MLA_V)
    zv = jnp.zeros((MLA_KV_RANK, MLA_HEADS // 2, MLA_V), F32)
    wv = jnp.stack([jnp.concatenate([uv[:, :, 0], zv], axis=2), jnp.concatenate([zv, uv[:, :, 1]], axis=2)], axis=2)
    hp = MLA_HEADS * MLA_HEAD_PAD
    b16 = lambda a: a.astype(BF16)
    return dict(
        ln1=ln1[l][None], ln2=ln2[l][None],
        w_dil=b16(wi[:, o[4]:o[7]]), w_gate=b16(0.5 * wi[:, o[7]:o[8]]), w1=b16(w1),
        q_norm=q_norm[l][None], kv_norm=kv_norm[l][None],
        wq=b16(wq.reshape(MLA_Q_RANK, hp)), wqs=b16(wqs.reshape(MLA_Q_RANK, hp)),
        wk=b16(wk.reshape(MLA_KV_RANK, hp)), wv=b16(wv.reshape(MLA_KV_RANK, hp)),
        pool_w=b16(pool_w[l]), pool_scale=pool_scale[l][None],
        w_a=b16(0.5 * w_a[l]), w_b=b16(0.5 * w_b[l]), w_c=b16(0.5 * w_c[l]), w_o=b16(w_o[l]),
        w_ff1=b16(w_ff1[l]), w_ff2=b16(w_ff2[l]),
    )


def _encoder_layer(x, B, S, p, tabs, gf, final):
    tm = min(512, S)
    outs = _in_proj(x, p, tabs, B, S, tm)
    qkv, (u, q, k, v) = outs[:N_DIL_OUT], outs[N_DIL_OUT:]
    dil = [_dil_attn(qkv[grp], qkv[DIL_GROUPS + grp], qkv[2 * DIL_GROUPS + grp], grp, 1024)
           for grp in range(DIL_GROUPS)]
    b = _mla_attn(q, k, v, B, S, min(256, S))
    x2 = _merge(x, u, b, [o for o, _ in dil], [l for _, l in dil], p, S, tm)
    return _ffn(x2, p["ln2"], p["w_ff1"], p["w_ff2"], gf, final, tm)


def kernel(x_prompt, x_sample, ln1, w_in, pool_w, pool_scale, q_norm, kv_norm, w_uq, w_uk, w_uv,
           w_a, w_b, w_c, w_o, ln2, w_ff1, w_ff2, final_norm):
    depth = w_in.shape[0]
    layers = [_layer_weights(l, ln1, w_in, pool_w, pool_scale, q_norm, kv_norm, w_uq, w_uk, w_uv,
                             w_a, w_b, w_c, w_o, ln2, w_ff1, w_ff2) for l in range(depth)]
    gf = final_norm[None]
    outs = []
    for x in (x_prompt, x_sample):
        B, S, D = x.shape
        tabs = _rope_tables(S)
        h = x.reshape(B * S, D)
        for l in range(depth):
            h = _encoder_layer(h, B, S, layers[l], tabs, gf, l == depth - 1)
        outs.append(h.reshape(B, S, D))
    return tuple(outs)
```

```python
import functools

import numpy as np
import jax
import jax.numpy as jnp
from jax import lax
from jax.experimental import pallas as pl
from jax.experimental.pallas import tpu as pltpu

F32 = jnp.float32
BF16 = jnp.bfloat16

D_MODEL = 1024
RMS_EPS = 1e-6
ROPE_THETA = 10000.0
NEG_INF = -1e30
LOG2_E = 1.4426950408889634

POOL_WIDTH = 512
POOL_WINDOWS = (2, 4, 8, 16)
POOL_GW = POOL_WIDTH // len(POOL_WINDOWS)
POOL_HALO = 8

MLA_HEADS = 8
MLA_NOPE = 64
MLA_ROPE = 32
MLA_V = 64
MLA_QK = MLA_NOPE + MLA_ROPE
MLA_Q_RANK = 384
MLA_KV_RANK = 256
MLA_HEAD_PAD = 128

DIL_PAIRS = ((128, 1), (512, 4), (2048, 16))
DIL_GROUPS = len(DIL_PAIRS)
DIL_HEADS = 4
DIL_HEAD_DIM = 128
DIL_GROUP_W = DIL_HEADS * DIL_HEAD_DIM
DIL_QKV = DIL_GROUPS * DIL_GROUP_W
DIL_HALF_W = 64
DIL_QBLK = 128
DIL_SPLIT = 4

N_BRANCH = 3
BRANCH_W = 512
D_FF = 4 * D_MODEL
FF_CHUNK = 1024

IN_SPLITS = (POOL_WIDTH, MLA_Q_RANK, MLA_KV_RANK, MLA_ROPE, DIL_QKV, DIL_QKV, DIL_QKV, N_BRANCH * D_MODEL)
IN_OFFSETS = [0] + [int(o) for o in np.cumsum(IN_SPLITS)]

VMEM_LIMIT = 56 * 1024 * 1024


def _params(*sem):
    return pltpu.CompilerParams(dimension_semantics=sem, vmem_limit_bytes=VMEM_LIMIT)


def _resident(shape):
    nd = len(shape)
    return pl.BlockSpec(shape, lambda *_: (0,) * nd, pipeline_mode=pl.Buffered(1))


def _rms(xf, g):
    return xf * lax.rsqrt(jnp.mean(xf * xf, axis=-1, keepdims=True) + RMS_EPS) * g


def _dot(a, b):
    return jnp.dot(a, b, preferred_element_type=F32)


def _dot_nt(a, b):
    return lax.dot_general(a, b, (((1,), (1,)), ((), ())), preferred_element_type=F32)


def _mla_proj(h, w1_ref, qn_ref, kvn_ref, wq_ref, wqs_ref, wk_ref, wv_ref, cos_ref, sin_ref,
              u_ref, q_ref, k_ref, v_ref):
    t = _dot(h, w1_ref[...])
    o_cq = POOL_WIDTH
    o_ckv = o_cq + MLA_Q_RANK
    o_kr = o_ckv + MLA_KV_RANK
    u_ref[...] = t[:, :o_cq]
    cqn = _rms(t[:, o_cq:o_ckv], qn_ref[...]).astype(BF16)
    cn = _rms(t[:, o_ckv:o_kr], kvn_ref[...]).astype(BF16)
    cos = cos_ref[...]
    sin = sin_ref[...]
    kr = t[:, o_kr:o_kr + MLA_HEAD_PAD] * cos + t[:, o_kr + MLA_HEAD_PAD:] * sin
    qf = _dot(cqn, wq_ref[...])
    qs = _dot(cqn, wqs_ref[...])
    kf = _dot(cn, wk_ref[...])
    scale = MLA_QK ** -0.5 * LOG2_E
    for hh in range(MLA_HEADS):
        sl = slice(hh * MLA_HEAD_PAD, (hh + 1) * MLA_HEAD_PAD)
        q_ref[:, sl] = ((qf[:, sl] * cos + qs[:, sl] * sin) * scale).astype(BF16)
        k_ref[:, sl] = (kf[:, sl] + kr).astype(BF16)
    lane = lax.broadcasted_iota(jnp.int32, (1, MLA_HEADS * MLA_HEAD_PAD), 1)
    ones_lane = jnp.where((lane // MLA_HEAD_PAD) % 2 == 0, MLA_V, 0)
    ones = (lane % MLA_HEAD_PAD == ones_lane).astype(F32)
    v_ref[...] = (_dot(cn, wv_ref[...]) + ones).astype(BF16)


def _dil_proj(h, w_ref, cos_ref, sin_ref, out_refs, stage, mid, tm):
    cos = cos_ref[...]
    sin = sin_ref[...]
    scale = DIL_HEAD_DIM ** -0.5 * LOG2_E
    n = 0
    for grp in reversed(range(DIL_GROUPS)):
        d = DIL_PAIRS[grp][1]
        for which in range(3):
            c = which * DIL_GROUPS + grp
            out = out_refs[c]
            acc = _dot(h, w_ref[:, c * DIL_GROUP_W:(c + 1) * DIL_GROUP_W])
            for hh in range(DIL_HEADS):
                sl = slice(hh * DIL_HEAD_DIM, (hh + 1) * DIL_HEAD_DIM)
                blk = acc[:, sl]
                if which < 2:
                    blk = blk * cos + pltpu.roll(blk, DIL_HEAD_DIM // 2, axis=1) * sin
                if which == 0:
                    blk = blk * scale
                if d == 1:
                    out[0, 0, :, sl] = blk.astype(BF16)
                else:
                    stage[n % 2, hh] = blk
            if d > 1:
                for hh in range(DIL_HEADS):
                    sl = slice(hh * DIL_HEAD_DIM, (hh + 1) * DIL_HEAD_DIM)
                    if d == DIL_SPLIT * DIL_SPLIT:
                        for lo in range(DIL_SPLIT):
                            mid[n % 2, hh, lo] = stage[n % 2, hh, pl.ds(lo, tm // DIL_SPLIT, stride=DIL_SPLIT), :]
                        for r in range(d):
                            hi, lo = divmod(r, DIL_SPLIT)
                            rows = pl.ds(hi, tm // d, stride=DIL_SPLIT)
                            out[0, r, :, sl] = mid[n % 2, hh, lo, rows, :].astype(BF16)
                    else:
                        for r in range(d):
                            out[0, r, :, sl] = stage[n % 2, hh, pl.ds(r, tm // d, stride=d), :].astype(BF16)
                n += 1


N_DIL_OUT = 3 * DIL_GROUPS


def _in_proj_kernel(x_ref, g_ref, wd_ref, dcos_ref, dsin_ref, w1_ref, qn_ref, kvn_ref, wq_ref, wqs_ref, wk_ref,
                    wv_ref, mcos_ref, msin_ref, *refs, tm):
    dil_out, (u_ref, q_ref, k_ref, v_ref), (stage, mid) = refs[:N_DIL_OUT], refs[N_DIL_OUT:N_DIL_OUT + 4], refs[-2:]
    h = _rms(x_ref[...], g_ref[...]).astype(BF16)
    _mla_proj(h, w1_ref, qn_ref, kvn_ref, wq_ref, wqs_ref, wk_ref, wv_ref, mcos_ref, msin_ref,
              u_ref, q_ref, k_ref, v_ref)
    _dil_proj(h, wd_ref, dcos_ref, dsin_ref, dil_out, stage, mid, tm)


def _in_proj(x, p, tabs, B, S, tm):
    (dcos, dsin), (mcos, msin) = tabs
    T = x.shape[0]
    nt = S // tm
    row = lambda i: (i, 0)
    tab = lambda i: (i % nt, 0)
    cls = lambda i: (i // nt, 0, i % nt, 0)
    hp = MLA_HEADS * MLA_HEAD_PAD
    ds_ = [DIL_PAIRS[c % DIL_GROUPS][1] for c in range(N_DIL_OUT)]
    weights = [p[n] for n in ("w1", "q_norm", "kv_norm", "wq", "wqs", "wk", "wv")]
    return pl.pallas_call(
        functools.partial(_in_proj_kernel, tm=tm),
        grid=(T // tm,),
        in_specs=[pl.BlockSpec((tm, D_MODEL), row), _resident((1, D_MODEL)), _resident(p["w_dil"].shape),
                  pl.BlockSpec((tm, DIL_HEAD_DIM), tab), pl.BlockSpec((tm, DIL_HEAD_DIM), tab)]
        + [_resident(w.shape) for w in weights]
        + [pl.BlockSpec((tm, MLA_HEAD_PAD), tab), pl.BlockSpec((tm, MLA_HEAD_PAD), tab)],
        out_specs=[pl.BlockSpec((1, d, tm // d, DIL_GROUP_W), cls) for d in ds_]
        + [pl.BlockSpec((tm, POOL_WIDTH), row)] + [pl.BlockSpec((tm, hp), row)] * 3,
        out_shape=[jax.ShapeDtypeStruct((B, d, S // d, DIL_GROUP_W), BF16) for d in ds_]
        + [jax.ShapeDtypeStruct((T, POOL_WIDTH), F32)] + [jax.ShapeDtypeStruct((T, hp), BF16)] * 3,
        scratch_shapes=[pltpu.VMEM((2, DIL_HEADS, tm, DIL_HEAD_DIM), F32),
                        pltpu.VMEM((2, DIL_HEADS, DIL_SPLIT, tm // DIL_SPLIT, DIL_HEAD_DIM), F32)],
        compiler_params=_params("parallel"),
        name="in_proj",
    )(x, p["ln1"], p["w_dil"], dcos, dsin, *weights, mcos, msin)


def _dil_attn_kernel(q_ref, kp_ref, kc_ref, kn_ref, vp_ref, vc_ref, vn_ref, o_ref, lse_ref, kwin, vwin,
                     *, tl, L, dr):
    i = pl.program_id(2)
    hw = DIL_HALF_W
    for rr in range(dr):
        kwin[rr, 0:hw] = kp_ref[0, rr]
        kwin[rr, hw:hw + tl] = kc_ref[0, rr]
        kwin[rr, hw + tl:] = kn_ref[0, rr]
        vwin[rr, 0:hw] = vp_ref[0, rr]
        vwin[rr, hw:hw + tl] = vc_ref[0, rr]
        vwin[rr, hw + tl:] = vn_ref[0, rr]
    nk = DIL_QBLK + 2 * hw
    row = lax.broadcasted_iota(jnp.int32, (DIL_QBLK, nk), 0)
    col = lax.broadcasted_iota(jnp.int32, (DIL_QBLK, nk), 1)
    band = (col >= row) & (col - row <= 2 * hw)
    lane = lax.broadcasted_iota(jnp.int32, (DIL_QBLK, 128), 1)
    head = lambda hh: slice(hh * DIL_HEAD_DIM, (hh + 1) * DIL_HEAD_DIM)
    units = [(j, rr, hh) for j in range(tl // DIL_QBLK) for rr in range(dr) for hh in range(DIL_HEADS)]

    def scores(u):
        j, rr, hh = u
        r0 = j * DIL_QBLK
        return _dot_nt(q_ref[0, rr, r0:r0 + DIL_QBLK, head(hh)], kwin[rr, r0:r0 + nk, head(hh)])

    s_next = scores(units[0])
    valid = lse_tile = None
    for n, (j, rr, hh) in enumerate(units):
        r0 = j * DIL_QBLK
        s = s_next
        if n + 1 < len(units):
            s_next = scores(units[n + 1])
        if rr == 0 and hh == 0:
            kidx = i * tl + r0 - hw + col
            valid = band & (kidx >= 0) & (kidx < L)
        if hh == 0:
            lse_tile = jnp.zeros((DIL_QBLK, 128), F32)
        s = jnp.where(valid, s, NEG_INF)
        m = jnp.max(s, axis=-1, keepdims=True)
        p = jnp.exp2(s - m)
        l = jnp.sum(p, axis=-1, keepdims=True)
        o = _dot(p.astype(BF16), vwin[rr, r0:r0 + nk, head(hh)]) / l
        o_ref[0, rr, r0:r0 + DIL_QBLK, head(hh)] = o.astype(BF16)
        lse_tile = jnp.where(lane == hh, m + jnp.log2(l), lse_tile)
        if hh == DIL_HEADS - 1:
            lse_ref[0, rr, r0:r0 + DIL_QBLK, :] = lse_tile


def _dil_attn(q, k, v, grp, rows):
    B, d, L, _ = q.shape
    tl = min(rows, L)
    dr = min(d, max(1, rows // tl))
    hb = tl // DIL_HALF_W
    nhb = L // DIL_HALF_W
    cur = lambda b, r, i: (b, r, i, 0)
    prev = lambda b, r, i: (b, r, jnp.maximum(i * hb - 1, 0), 0)
    nxt = lambda b, r, i: (b, r, jnp.minimum((i + 1) * hb, nhb - 1), 0)
    blk = lambda n, f: pl.BlockSpec((1, dr, n, DIL_GROUP_W), f)
    return pl.pallas_call(
        functools.partial(_dil_attn_kernel, tl=tl, L=L, dr=dr),
        grid=(B, d // dr, L // tl),
        in_specs=[blk(tl, cur), blk(DIL_HALF_W, prev), blk(tl, cur), blk(DIL_HALF_W, nxt),
                  blk(DIL_HALF_W, prev), blk(tl, cur), blk(DIL_HALF_W, nxt)],
        out_specs=[pl.BlockSpec((1, dr, tl, DIL_GROUP_W), cur), pl.BlockSpec((1, dr, tl, 128), cur)],
        out_shape=[jax.ShapeDtypeStruct((B, d, L, DIL_GROUP_W), BF16),
                   jax.ShapeDtypeStruct((B, d, L, 128), F32)],
        scratch_shapes=[pltpu.VMEM((dr, tl + 2 * DIL_HALF_W, DIL_GROUP_W), BF16)] * 2,
        compiler_params=_params("parallel", "parallel", "parallel"),
        name=f"dil_attn_g{grp}",
    )(q, k, k, k, v, v, v)


MLA_HEADS_PER_STEP = 4


def _mla_attn_kernel(q_ref, k_ref, v_ref, o_ref, *, tq):
    lane = lax.broadcasted_iota(jnp.int32, (tq, 2 * MLA_V), 1)
    head = lambda h: slice(h * MLA_HEAD_PAD, (h + 1) * MLA_HEAD_PAD)

    def tile(t, carry):
        rows = pl.ds(pl.multiple_of(t * tq, tq), tq)
        scores = lambda h: _dot_nt(q_ref[rows, head(h)], k_ref[:, head(h)])
        outs = []
        s_next = scores(0)
        for h in range(MLA_HEADS_PER_STEP):
            s = s_next
            if h + 1 < MLA_HEADS_PER_STEP:
                s_next = scores(h + 1)
            m = jnp.max(s, axis=-1, keepdims=True)
            p = jnp.exp2(s - m).astype(BF16)
            ov = _dot(p, v_ref[:, head(h)])
            lcol = MLA_V if h % 2 == 0 else 0
            outs.append(ov / ov[:, lcol:lcol + 1])
        for pr in range(MLA_HEADS_PER_STEP // 2):
            pair = jnp.where(lane < MLA_V, outs[2 * pr], outs[2 * pr + 1])
            o_ref[rows, pr * 2 * MLA_V:(pr + 1) * 2 * MLA_V] = pair.astype(BF16)
        return carry

    lax.fori_loop(0, q_ref.shape[0] // tq, tile, 0)


def _mla_attn(q, k, v, B, S, tq, tiles):
    T = B * S
    bq = tq * tiles
    nq = S // bq
    nh = MLA_HEADS_PER_STEP
    kv_bytes = 2 * S * nh * MLA_HEAD_PAD * 2
    mode = {} if 2 * kv_bytes <= VMEM_LIMIT // 2 else dict(pipeline_mode=pl.Buffered(1))
    kv = pl.BlockSpec((S, nh * MLA_HEAD_PAD), lambda b, p, i: (b, p), **mode)
    return pl.pallas_call(
        functools.partial(_mla_attn_kernel, tq=tq),
        grid=(B, MLA_HEADS // nh, nq),
        in_specs=[pl.BlockSpec((bq, nh * MLA_HEAD_PAD), lambda b, p, i: (b * nq + i, p)), kv, kv],
        out_specs=pl.BlockSpec((bq, nh * MLA_V), lambda b, p, i: (b * nq + i, p)),
        out_shape=jax.ShapeDtypeStruct((T, MLA_HEADS * MLA_V), BF16),
        compiler_params=_params("parallel", "parallel", "parallel"),
        name="mla_attn",
    )(q, k, v)


def _pool_fill(up_ref, uc_ref, un_ref, ext, iseq, nt, tm):
    hl = POOL_HALO
    ext[0:hl] = jnp.where(iseq > 0, up_ref[...], 0.0)
    ext[hl:hl + tm] = uc_ref[...]
    ext[hl + tm:] = jnp.where(iseq < nt - 1, un_ref[...], 0.0)


def _pool_group(g, uc_ref, w_ref, sc_ref, cnt_ref, ext, tm):
    hl, w = POOL_HALO, POOL_WINDOWS[g]
    cols = slice(g * POOL_GW, (g + 1) * POOL_GW)
    acc = ext[hl - w // 2:hl - w // 2 + tm, cols]
    for off in range(-w // 2 + 1, w // 2):
        acc = acc + ext[hl + off:hl + off + tm, cols]
    mixed = acc / cnt_ref[:, g:g + 1] - uc_ref[:, cols]
    return _dot(mixed.astype(BF16), w_ref[g]) * sc_ref[:, cols]


def _dil_to_token_order(grp, o_ref, l_ref, o_nat, l_nat, tm):
    d = DIL_PAIRS[grp][1]
    for r in range(d):
        rows = pl.ds(r, tm // d, stride=d) if d > 1 else slice(None)
        l_nat[grp, rows, :] = l_ref[0, r]
        for hh in range(DIL_HEADS):
            sl = slice(hh * DIL_HEAD_DIM, (hh + 1) * DIL_HEAD_DIM)
            o_nat[grp, hh, rows, :] = o_ref[0, r, :, sl].astype(F32)


def _dil_combine_head(hh, o_nat, l_nat):
    ls = [l_nat[grp, :, hh:hh + 1] for grp in range(DIL_GROUPS)]
    mx = jnp.maximum(jnp.maximum(ls[0], ls[1]), ls[2])
    es = [jnp.exp2(l - mx) for l in ls]
    den = es[0] + es[1] + es[2]
    return sum((e / den) * o_nat[grp, hh] for grp, e in enumerate(es))


def _gate2(z_half):
    return jnp.tanh(z_half) + 1.0


def _merge_kernel(x_ref, g_ref, up_ref, uc_ref, un_ref, cnt_ref, b_ref, o0_ref, o1_ref, o2_ref, l0_ref, l1_ref, l2_ref,
                  pw_ref, ps_ref, wg_ref, wa_ref, wb_ref, wc_ref, wo_ref, x2_ref, ext, o_nat, l_nat, *, tm, nt):
    x = x_ref[...]
    h = _rms(x, g_ref[...]).astype(BF16)
    iseq = pl.program_id(0) % nt
    hw = D_MODEL // 2
    cols = lambda hf: slice(hf * hw, (hf + 1) * hw)
    logits = lambda j, hf: _dot(h, wg_ref[:, j * D_MODEL + hf * hw:j * D_MODEL + (hf + 1) * hw])
    pool = lambda g: _pool_group(g, uc_ref, pw_ref, ps_ref, cnt_ref, ext, tm)
    o_refs, l_refs = (o0_ref, o1_ref, o2_ref), (l0_ref, l1_ref, l2_ref)
    b = b_ref[...]

    l1a = logits(1, 0)
    _pool_fill(up_ref, uc_ref, un_ref, ext, iseq, nt, tm)
    a0 = pool(0)
    tba = _dot(b, wb_ref[:, cols(0)])
    a1 = pool(1)
    l1b = logits(1, 1)
    a2 = pool(2)
    tbb = _dot(b, wb_ref[:, cols(1)])
    l0a = logits(0, 0)
    a3 = pool(3)
    a = jnp.concatenate([a0, a1, a2, a3], axis=1).astype(BF16)
    l0b = logits(0, 1)
    _dil_to_token_order(0, o_refs[0], l_refs[0], o_nat, l_nat, tm)
    _dil_to_token_order(1, o_refs[1], l_refs[1], o_nat, l_nat, tm)
    taa = _dot(a, wa_ref[:, cols(0)])
    _dil_to_token_order(2, o_refs[2], l_refs[2], o_nat, l_nat, tm)
    l2a = logits(2, 0)
    c01 = [_dil_combine_head(hh, o_nat, l_nat) for hh in (0, 1)]
    tab = _dot(a, wa_ref[:, cols(1)])
    c23 = [_dil_combine_head(hh, o_nat, l_nat) for hh in (2, 3)]
    c = jnp.concatenate(c01 + c23, axis=1).astype(BF16)
    l2b = logits(2, 1)
    gba = _gate2(l1a) * tba + _gate2(l0a) * taa
    tca = _dot(c, wc_ref[:, cols(0)])
    gbb = _gate2(l1b) * tbb + _gate2(l0b) * tab
    tcb = _dot(c, wc_ref[:, cols(1)])
    ma = (gba + _gate2(l2a) * tca).astype(BF16)
    acc = x + _dot(ma, wo_ref[cols(0), :])
    mb = (gbb + _gate2(l2b) * tcb).astype(BF16)
    x2_ref[...] = acc + _dot(mb, wo_ref[cols(1), :])


def _merge(x, u, cnt, b, os_, ls_, p, S, tm):
    T = x.shape[0]
    nt = S // tm
    hb = tm // POOL_HALO
    nhb = T // POOL_HALO
    row = lambda i: (i, 0)
    cls = lambda i: (i // nt, 0, i % nt, 0)
    br = pl.BlockSpec((tm, BRANCH_W), row)
    halo = lambda f: pl.BlockSpec((POOL_HALO, POOL_WIDTH), f)
    ds_ = [d for _, d in DIL_PAIRS]
    weights = [p[n] for n in ("pool_w", "pool_scale", "w_gate", "w_a", "w_b", "w_c", "w_o")]
    return pl.pallas_call(
        functools.partial(_merge_kernel, tm=tm, nt=nt),
        grid=(T // tm,),
        in_specs=[pl.BlockSpec((tm, D_MODEL), row), _resident((1, D_MODEL)),
                  halo(lambda i: (jnp.maximum(i * hb - 1, 0), 0)), br,
                  halo(lambda i: (jnp.minimum((i + 1) * hb, nhb - 1), 0)),
                  pl.BlockSpec((tm, 128), lambda i: (i % nt, 0)), br]
        + [pl.BlockSpec((1, d, tm // d, DIL_GROUP_W), cls) for d in ds_]
        + [pl.BlockSpec((1, d, tm // d, 128), cls) for d in ds_]
        + [_resident(w.shape) for w in weights],
        out_specs=pl.BlockSpec((tm, D_MODEL), row),
        out_shape=jax.ShapeDtypeStruct((T, D_MODEL), F32),
        scratch_shapes=[pltpu.VMEM((tm + 2 * POOL_HALO, POOL_WIDTH), F32),
                        pltpu.VMEM((DIL_GROUPS, DIL_HEADS, tm, DIL_HEAD_DIM), F32),
                        pltpu.VMEM((DIL_GROUPS, tm, 128), F32)],
        compiler_params=_params("parallel"),
        name="merge",
    )(x, p["ln1"], u, u, u, cnt, b, *os_, *ls_, *weights)


def _ffn_kernel(x_ref, g_ref, w1_ref, w2_ref, gf_ref, y_ref, *, final):
    x = x_ref[...]
    h = _rms(x, g_ref[...]).astype(BF16)
    acc = x
    for c in range(D_FF // FF_CHUNK):
        sl = slice(c * FF_CHUNK, (c + 1) * FF_CHUNK)
        f = jnp.maximum(_dot(h, w1_ref[:, sl]), 0.0)
        acc = acc + _dot((f * f).astype(BF16), w2_ref[sl, :])
    if final:
        acc = _rms(acc, gf_ref[...])
    y_ref[...] = acc


def _ffn(x, g, w1, w2, gf, final, tm):
    T = x.shape[0]
    row = lambda i: (i, 0)
    return pl.pallas_call(
        functools.partial(_ffn_kernel, final=final),
        grid=(T // tm,),
        in_specs=[pl.BlockSpec((tm, D_MODEL), row), _resident((1, D_MODEL)), _resident(w1.shape),
                  _resident(w2.shape), _resident((1, D_MODEL))],
        out_specs=pl.BlockSpec((tm, D_MODEL), row),
        out_shape=jax.ShapeDtypeStruct((T, D_MODEL), F32),
        compiler_params=_params("parallel"),
        name="ffn",
    )(x, g, w1, w2, gf)


def _rope_tables(S):
    pos = jnp.arange(S).astype(F32)[:, None]

    def cs(half):
        inv = ROPE_THETA ** (-jnp.arange(half, dtype=F32) / half)
        ang = pos * inv[None, :]
        return jnp.cos(ang), jnp.sin(ang)

    c, s = cs(DIL_HEAD_DIM // 2)
    dil = (jnp.concatenate([c, c], axis=1), jnp.concatenate([-s, s], axis=1))
    c, s = cs(MLA_ROPE // 2)
    pad = MLA_HEAD_PAD - MLA_QK
    mla = (jnp.concatenate([jnp.ones((S, MLA_NOPE), F32), c, c, jnp.zeros((S, pad), F32)], axis=1),
           jnp.concatenate([jnp.zeros((S, MLA_NOPE), F32), -s, s, jnp.zeros((S, pad), F32)], axis=1))
    return dil, mla


def _pool_counts(S):
    t = jnp.arange(S)[:, None]
    half = jnp.asarray([w // 2 for w in POOL_WINDOWS] + [1] * (128 - len(POOL_WINDOWS)))[None, :]
    return (jnp.clip(t + half, 0, S) - jnp.clip(t - half, 0, S)).astype(F32)


def _layer_weights(l, ln1, w_in, pool_w, pool_scale, q_norm, kv_norm, w_uq, w_uk, w_uv,
                   w_a, w_b, w_c, w_o, ln2, w_ff1, w_ff2):
    o = IN_OFFSETS
    wi = w_in[l]
    half = MLA_ROPE // 2
    pad = MLA_HEAD_PAD - MLA_QK
    kr = wi[:, o[3]:o[4]]
    zl = jnp.zeros((D_MODEL, MLA_NOPE), F32)
    zr = jnp.zeros((D_MODEL, pad), F32)
    kr_placed = jnp.concatenate([zl, kr[:, :half], kr[:, half:], zr], axis=1)
    kr_swapped = jnp.concatenate([zl, kr[:, half:], kr[:, :half], zr], axis=1)
    w1 = jnp.concatenate([wi[:, o[0]:o[3]], kr_placed, kr_swapped], axis=1)
    uq = w_uq[l].reshape(MLA_Q_RANK, MLA_HEADS, MLA_QK)
    zq = jnp.zeros((MLA_Q_RANK, MLA_HEADS, pad), F32)
    wq = jnp.concatenate([uq, zq], axis=2)
    wqs = jnp.concatenate([jnp.zeros((MLA_Q_RANK, MLA_HEADS, MLA_NOPE), F32), uq[:, :, MLA_NOPE + half:],
                           uq[:, :, MLA_NOPE:MLA_NOPE + half], zq], axis=2)
    uk = w_uk[l].reshape(MLA_KV_RANK, MLA_HEADS, MLA_NOPE)
    wk = jnp.concatenate([uk, jnp.zeros((MLA_KV_RANK, MLA_HEADS, MLA_HEAD_PAD - MLA_NOPE), F32)], axis=2)
    uv = w_uv[l].reshape(MLA_KV_RANK, MLA_HEADS // 2, 2, MLA_V)
    zv = jnp.zeros((MLA_KV_RANK, MLA_HEADS // 2, MLA_V), F32)
    wv = jnp.stack([jnp.concatenate([uv[:, :, 0], zv], axis=2), jnp.concatenate([zv, uv[:, :, 1]], axis=2)], axis=2)
    hp = MLA_HEADS * MLA_HEAD_PAD
    b16 = lambda a: a.astype(BF16)
    return dict(
        ln1=ln1[l][None], ln2=ln2[l][None],
        w_dil=b16(wi[:, o[4]:o[7]]), w_gate=b16(0.5 * wi[:, o[7]:o[8]]), w1=b16(w1),
        q_norm=q_norm[l][None], kv_norm=kv_norm[l][None],
        wq=b16(wq.reshape(MLA_Q_RANK, hp)), wqs=b16(wqs.reshape(MLA_Q_RANK, hp)),
        wk=b16(wk.reshape(MLA_KV_RANK, hp)), wv=b16(wv.reshape(MLA_KV_RANK, hp)),
        pool_w=b16(pool_w[l]), pool_scale=pool_scale[l][None],
        w_a=b16(0.5 * w_a[l]), w_b=b16(0.5 * w_b[l]), w_c=b16(0.5 * w_c[l]), w_o=b16(w_o[l]),
        w_ff1=b16(w_ff1[l]), w_ff2=b16(w_ff2[l]),
    )


def _encoder_layer(x, B, S, p, tabs, cnt, gf, final):
    tm = min(512, S)
    outs = _in_proj(x, p, tabs, B, S, tm)
    qkv, (u, q, k, v) = outs[:N_DIL_OUT], outs[N_DIL_OUT:]
    dil = [_dil_attn(qkv[grp], qkv[DIL_GROUPS + grp], qkv[2 * DIL_GROUPS + grp], grp, 1024)
           for grp in range(DIL_GROUPS)]
    b = _mla_attn(q, k, v, B, S, min(256, S), 4)
    x2 = _merge(x, u, cnt, b, [o for o, _ in dil], [l for _, l in dil], p, S, tm)
    return _ffn(x2, p["ln2"], p["w_ff1"], p["w_ff2"], gf, final, tm)


def kernel(x_prompt, x_sample, ln1, w_in, pool_w, pool_scale, q_norm, kv_norm, w_uq, w_uk, w_uv,
           w_a, w_b, w_c, w_o, ln2, w_ff1, w_ff2, final_norm):
    depth = w_in.shape[0]
    layers = [_layer_weights(l, ln1, w_in, pool_w, pool_scale, q_norm, kv_norm, w_uq, w_uk, w_uv,
                             w_a, w_b, w_c, w_o, ln2, w_ff1, w_ff2) for l in range(depth)]
    gf = final_norm[None]
    outs = []
    for x in (x_prompt, x_sample):
        B, S, D = x.shape
        tabs = _rope_tables(S)
        cnt = _pool_counts(S)
        h = x.reshape(B * S, D)
        for l in range(depth):
            h = _encoder_layer(h, B, S, layers[l], tabs, cnt, gf, l == depth - 1)
        outs.append(h.reshape(B, S, D))
    return tuple(outs)
```

```python
import functools

import numpy as np
import jax
import jax.numpy as jnp
from jax import lax
from jax.experimental import pallas as pl
from jax.experimental.pallas import tpu as pltpu

F32 = jnp.float32
BF16 = jnp.bfloat16

D_MODEL = 1024
RMS_EPS = 1e-6
ROPE_THETA = 10000.0
NEG_INF = -1e30
LOG2_E = 1.4426950408889634

POOL_WIDTH = 512
POOL_WINDOWS = (2, 4, 8, 16)
POOL_GW = POOL_WIDTH // len(POOL_WINDOWS)
POOL_HALO = 8

MLA_HEADS = 8
MLA_NOPE = 64
MLA_ROPE = 32
MLA_V = 64
MLA_QK = MLA_NOPE + MLA_ROPE
MLA_Q_RANK = 384
MLA_KV_RANK = 256
MLA_HEAD_PAD = 128

DIL_PAIRS = ((128, 1), (512, 4), (2048, 16))
DIL_GROUPS = len(DIL_PAIRS)
DIL_HEADS = 4
DIL_HEAD_DIM = 128
DIL_GROUP_W = DIL_HEADS * DIL_HEAD_DIM
DIL_QKV = DIL_GROUPS * DIL_GROUP_W
DIL_HALF_W = 64
DIL_QBLK = 128
DIL_AHEAD = 2
DIL_SPLIT = 4

N_BRANCH = 3
BRANCH_W = 512
D_FF = 4 * D_MODEL
FF_CHUNK = 1024

IN_SPLITS = (POOL_WIDTH, MLA_Q_RANK, MLA_KV_RANK, MLA_ROPE, DIL_QKV, DIL_QKV, DIL_QKV, N_BRANCH * D_MODEL)
IN_OFFSETS = [0] + [int(o) for o in np.cumsum(IN_SPLITS)]

VMEM_LIMIT = 56 * 1024 * 1024


def _params(*sem):
    return pltpu.CompilerParams(dimension_semantics=sem, vmem_limit_bytes=VMEM_LIMIT)


def _resident(shape):
    nd = len(shape)
    return pl.BlockSpec(shape, lambda *_: (0,) * nd, pipeline_mode=pl.Buffered(1))


def _rms(xf, g):
    return xf * lax.rsqrt(jnp.mean(xf * xf, axis=-1, keepdims=True) + RMS_EPS) * g


def _dot(a, b):
    return jnp.dot(a, b, preferred_element_type=F32)


def _dot_nt(a, b):
    return lax.dot_general(a, b, (((1,), (1,)), ((), ())), preferred_element_type=F32)


def _mla_proj(h, w1_ref, qn_ref, kvn_ref, wq_ref, wqs_ref, wk_ref, wv_ref, cos, sin, u_ref, qkv_ref):
    t = _dot(h, w1_ref[...])
    o_cq = POOL_WIDTH
    o_ckv = o_cq + MLA_Q_RANK
    o_kr = o_ckv + MLA_KV_RANK
    hp = MLA_HEADS * MLA_HEAD_PAD
    u_ref[...] = t[:, :o_cq]
    cqn = _rms(t[:, o_cq:o_ckv], qn_ref[...]).astype(BF16)
    cn = _rms(t[:, o_ckv:o_kr], kvn_ref[...]).astype(BF16)
    kr = t[:, o_kr:o_kr + MLA_HEAD_PAD] * cos + t[:, o_kr + MLA_HEAD_PAD:] * sin
    qf = _dot(cqn, wq_ref[...])
    qs = _dot(cqn, wqs_ref[...])
    kf = _dot(cn, wk_ref[...])
    scale = MLA_QK ** -0.5 * LOG2_E
    for hh in range(MLA_HEADS):
        sl = slice(hh * MLA_HEAD_PAD, (hh + 1) * MLA_HEAD_PAD)
        qkv_ref[:, sl] = ((qf[:, sl] * cos + qs[:, sl] * sin) * scale).astype(BF16)
        qkv_ref[:, hp + sl.start:hp + sl.stop] = (kf[:, sl] + kr).astype(BF16)
    lane = lax.broadcasted_iota(jnp.int32, (1, hp), 1)
    ones_lane = jnp.where((lane // MLA_HEAD_PAD) % 2 == 0, MLA_V, 0)
    ones = (lane % MLA_HEAD_PAD == ones_lane).astype(F32)
    qkv_ref[:, 2 * hp:] = (_dot(cn, wv_ref[...]) + ones).astype(BF16)


def _dil_proj(h, w_ref, cos, sin, out_refs, stage, mid, tm):
    scale = DIL_HEAD_DIM ** -0.5 * LOG2_E
    n = 0
    for grp in reversed(range(DIL_GROUPS)):
        d = DIL_PAIRS[grp][1]
        out = out_refs[grp]
        for which in range(3):
            c = which * DIL_GROUPS + grp
            acc = _dot(h, w_ref[:, c * DIL_GROUP_W:(c + 1) * DIL_GROUP_W])
            for hh in range(DIL_HEADS):
                sl = slice(hh * DIL_HEAD_DIM, (hh + 1) * DIL_HEAD_DIM)
                dst = slice(which * DIL_GROUP_W + sl.start, which * DIL_GROUP_W + sl.stop)
                blk = acc[:, sl]
                if which < 2:
                    blk = blk * cos + pltpu.roll(blk, DIL_HEAD_DIM // 2, axis=1) * sin
                if which == 0:
                    blk = blk * scale
                if d == 1:
                    out[0, 0, :, dst] = blk.astype(BF16)
                else:
                    stage[n % 2, hh] = blk
            if d > 1:
                for hh in range(DIL_HEADS):
                    sl = slice(hh * DIL_HEAD_DIM, (hh + 1) * DIL_HEAD_DIM)
                    dst = slice(which * DIL_GROUP_W + sl.start, which * DIL_GROUP_W + sl.stop)
                    if d == DIL_SPLIT * DIL_SPLIT:
                        for lo in range(DIL_SPLIT):
                            mid[n % 2, hh, lo] = stage[n % 2, hh, pl.ds(lo, tm // DIL_SPLIT, stride=DIL_SPLIT), :]
                        for r in range(d):
                            hi, lo = divmod(r, DIL_SPLIT)
                            rows = pl.ds(hi, tm // d, stride=DIL_SPLIT)
                            out[0, r, :, dst] = mid[n % 2, hh, lo, rows, :].astype(BF16)
                    else:
                        for r in range(d):
                            out[0, r, :, dst] = stage[n % 2, hh, pl.ds(r, tm // d, stride=d), :].astype(BF16)
                n += 1


def _in_proj_kernel(x_ref, g_ref, wd_ref, tab_ref, w1_ref, qn_ref, kvn_ref, wq_ref, wqs_ref, wk_ref, wv_ref,
                    *refs, tm):
    dil_out, (u_ref, qkv_ref), (stage, mid) = refs[:DIL_GROUPS], refs[DIL_GROUPS:DIL_GROUPS + 2], refs[-2:]
    h = _rms(x_ref[...], g_ref[...]).astype(BF16)
    tab = lambda n: tab_ref[:, n * 128:(n + 1) * 128]
    _mla_proj(h, w1_ref, qn_ref, kvn_ref, wq_ref, wqs_ref, wk_ref, wv_ref, tab(2), tab(3), u_ref, qkv_ref)
    _dil_proj(h, wd_ref, tab(0), tab(1), dil_out, stage, mid, tm)


def _in_proj(x, p, tabs, B, S, tm):
    T = x.shape[0]
    nt = S // tm
    row = lambda i: (i, 0)
    cls = lambda i: (i // nt, 0, i % nt, 0)
    hp = MLA_HEADS * MLA_HEAD_PAD
    ds_ = [d for _, d in DIL_PAIRS]
    weights = [p[n] for n in ("w1", "q_norm", "kv_norm", "wq", "wqs", "wk", "wv")]
    return pl.pallas_call(
        functools.partial(_in_proj_kernel, tm=tm),
        grid=(T // tm,),
        in_specs=[pl.BlockSpec((tm, D_MODEL), row), _resident((1, D_MODEL)), _resident(p["w_dil"].shape),
                  pl.BlockSpec((tm, tabs.shape[1]), lambda i: (i % nt, 0))]
        + [_resident(w.shape) for w in weights],
        out_specs=[pl.BlockSpec((1, d, tm // d, DIL_QKV), cls) for d in ds_]
        + [pl.BlockSpec((tm, POOL_WIDTH), row), pl.BlockSpec((tm, 3 * hp), row)],
        out_shape=[jax.ShapeDtypeStruct((B, d, S // d, DIL_QKV), BF16) for d in ds_]
        + [jax.ShapeDtypeStruct((T, POOL_WIDTH), F32), jax.ShapeDtypeStruct((T, 3 * hp), BF16)],
        scratch_shapes=[pltpu.VMEM((2, DIL_HEADS, tm, DIL_HEAD_DIM), F32),
                        pltpu.VMEM((2, DIL_HEADS, DIL_SPLIT, tm // DIL_SPLIT, DIL_HEAD_DIM), F32)],
        compiler_params=_params("parallel"),
        name="in_proj",
    )(x, p["ln1"], p["w_dil"], tabs, *weights)


def _dil_attn_kernel(q_ref, kp_ref, kc_ref, kn_ref, vp_ref, vc_ref, vn_ref, o_ref, lse_ref, kwin, vwin,
                     *, tl, L, dr):
    i = pl.program_id(2)
    hw = DIL_HALF_W
    for rr in range(dr):
        kwin[rr, 0:hw] = kp_ref[0, rr]
        kwin[rr, hw:hw + tl] = kc_ref[0, rr]
        kwin[rr, hw + tl:] = kn_ref[0, rr]
        vwin[rr, 0:hw] = vp_ref[0, rr]
        vwin[rr, hw:hw + tl] = vc_ref[0, rr]
        vwin[rr, hw + tl:] = vn_ref[0, rr]
    nk = DIL_QBLK + 2 * hw
    row = lax.broadcasted_iota(jnp.int32, (DIL_QBLK, nk), 0)
    col = lax.broadcasted_iota(jnp.int32, (DIL_QBLK, nk), 1)
    band = (col >= row) & (col - row <= 2 * hw)
    lane = lax.broadcasted_iota(jnp.int32, (DIL_QBLK, 128), 1)
    head = lambda hh: slice(hh * DIL_HEAD_DIM, (hh + 1) * DIL_HEAD_DIM)
    units = [(j, rr, hh) for j in range(tl // DIL_QBLK) for rr in range(dr) for hh in range(DIL_HEADS)]

    def scores(u):
        j, rr, hh = u
        r0 = j * DIL_QBLK
        return _dot_nt(q_ref[0, rr, r0:r0 + DIL_QBLK, head(hh)], kwin[rr, r0:r0 + nk, head(hh)])

    ahead = [scores(u) for u in units[:DIL_AHEAD]]
    valid = lse_tile = None
    for n, (j, rr, hh) in enumerate(units):
        r0 = j * DIL_QBLK
        s = ahead.pop(0)
        if n + DIL_AHEAD < len(units):
            ahead.append(scores(units[n + DIL_AHEAD]))
        if rr == 0 and hh == 0:
            kidx = i * tl + r0 - hw + col
            valid = band & (kidx >= 0) & (kidx < L)
        if hh == 0:
            lse_tile = jnp.zeros((DIL_QBLK, 128), F32)
        s = jnp.where(valid, s, NEG_INF)
        m = jnp.max(s, axis=-1, keepdims=True)
        p = jnp.exp2(s - m)
        l = jnp.sum(p, axis=-1, keepdims=True)
        o = _dot(p.astype(BF16), vwin[rr, r0:r0 + nk, head(hh)]) / l
        o_ref[0, rr, r0:r0 + DIL_QBLK, head(hh)] = o.astype(BF16)
        lse_tile = jnp.where(lane == hh, m + jnp.log2(l), lse_tile)
        if hh == DIL_HEADS - 1:
            lse_ref[0, rr, r0:r0 + DIL_QBLK, :] = lse_tile


def _dil_attn(qkv, grp, rows):
    B, d, L, _ = qkv.shape
    tl = min(rows, L)
    dr = min(d, max(1, rows // tl))
    hb = tl // DIL_HALF_W
    nhb = L // DIL_HALF_W
    cur = lambda w: lambda b, r, i: (b, r, i, w)
    prev = lambda w: lambda b, r, i: (b, r, jnp.maximum(i * hb - 1, 0), w)
    nxt = lambda w: lambda b, r, i: (b, r, jnp.minimum((i + 1) * hb, nhb - 1), w)
    blk = lambda n, f: pl.BlockSpec((1, dr, n, DIL_GROUP_W), f)
    return pl.pallas_call(
        functools.partial(_dil_attn_kernel, tl=tl, L=L, dr=dr),
        grid=(B, d // dr, L // tl),
        in_specs=[blk(tl, cur(0)), blk(DIL_HALF_W, prev(1)), blk(tl, cur(1)), blk(DIL_HALF_W, nxt(1)),
                  blk(DIL_HALF_W, prev(2)), blk(tl, cur(2)), blk(DIL_HALF_W, nxt(2))],
        out_specs=[pl.BlockSpec((1, dr, tl, DIL_GROUP_W), cur(0)), pl.BlockSpec((1, dr, tl, 128), cur(0))],
        out_shape=[jax.ShapeDtypeStruct((B, d, L, DIL_GROUP_W), BF16),
                   jax.ShapeDtypeStruct((B, d, L, 128), F32)],
        scratch_shapes=[pltpu.VMEM((dr, tl + 2 * DIL_HALF_W, DIL_GROUP_W), BF16)] * 2,
        compiler_params=_params("parallel", "parallel", "parallel"),
        name=f"dil_attn_g{grp}",
    )(*([qkv] * 7))


MLA_HEADS_PER_STEP = 4


def _mla_attn_kernel(q_ref, k_ref, v_ref, o_ref, *, tq):
    lane = lax.broadcasted_iota(jnp.int32, (tq, 2 * MLA_V), 1)
    head = lambda h: slice(h * MLA_HEAD_PAD, (h + 1) * MLA_HEAD_PAD)

    def tile(t, carry):
        rows = pl.ds(pl.multiple_of(t * tq, tq), tq)
        scores = lambda h: _dot_nt(q_ref[rows, head(h)], k_ref[:, head(h)])
        outs = []
        s_next = scores(0)
        for h in range(MLA_HEADS_PER_STEP):
            s = s_next
            if h + 1 < MLA_HEADS_PER_STEP:
                s_next = scores(h + 1)
            m = jnp.max(s, axis=-1, keepdims=True)
            p = jnp.exp2(s - m).astype(BF16)
            ov = _dot(p, v_ref[:, head(h)])
            lcol = MLA_V if h % 2 == 0 else 0
            outs.append(ov / ov[:, lcol:lcol + 1])
        for pr in range(MLA_HEADS_PER_STEP // 2):
            pair = jnp.where(lane < MLA_V, outs[2 * pr], outs[2 * pr + 1])
            o_ref[rows, pr * 2 * MLA_V:(pr + 1) * 2 * MLA_V] = pair.astype(BF16)
        return carry

    lax.fori_loop(0, q_ref.shape[0] // tq, tile, 0)


def _mla_attn(qkv, B, S, tq, tiles):
    T = B * S
    bq = tq * tiles
    nq = S // bq
    nh = MLA_HEADS_PER_STEP
    ng = MLA_HEADS // nh
    kv_bytes = 2 * S * nh * MLA_HEAD_PAD * 2
    mode = {} if 2 * kv_bytes <= VMEM_LIMIT // 2 else dict(pipeline_mode=pl.Buffered(1))
    kv = lambda which: pl.BlockSpec((S, nh * MLA_HEAD_PAD), lambda b, p, i: (b, which * ng + p), **mode)
    return pl.pallas_call(
        functools.partial(_mla_attn_kernel, tq=tq),
        grid=(B, ng, nq),
        in_specs=[pl.BlockSpec((bq, nh * MLA_HEAD_PAD), lambda b, p, i: (b * nq + i, p)), kv(1), kv(2)],
        out_specs=pl.BlockSpec((bq, nh * MLA_V), lambda b, p, i: (b * nq + i, p)),
        out_shape=jax.ShapeDtypeStruct((T, MLA_HEADS * MLA_V), BF16),
        compiler_params=_params("parallel", "parallel", "parallel"),
        name="mla_attn",
    )(qkv, qkv, qkv)


def _pool_fill(up_ref, uc_ref, un_ref, ext, iseq, nt, tm):
    hl = POOL_HALO
    ext[0:hl] = jnp.where(iseq > 0, up_ref[...], 0.0)
    ext[hl:hl + tm] = uc_ref[...]
    ext[hl + tm:] = jnp.where(iseq < nt - 1, un_ref[...], 0.0)


def _pool_group(g, uc_ref, w_ref, sc_ref, cnt_ref, ext, tm):
    hl, w = POOL_HALO, POOL_WINDOWS[g]
    cols = slice(g * POOL_GW, (g + 1) * POOL_GW)
    acc = ext[hl - w // 2:hl - w // 2 + tm, cols]
    for off in range(-w // 2 + 1, w // 2):
        acc = acc + ext[hl + off:hl + off + tm, cols]
    mixed = acc / cnt_ref[:, g:g + 1] - uc_ref[:, cols]
    return _dot(mixed.astype(BF16), w_ref[g]) * sc_ref[:, cols]


def _dil_to_token_order(grp, o_ref, l_ref, o_nat, l_nat, tm):
    d = DIL_PAIRS[grp][1]
    for r in range(d):
        rows = pl.ds(r, tm // d, stride=d) if d > 1 else slice(None)
        l_nat[grp, rows, :] = l_ref[0, r]
        for hh in range(DIL_HEADS):
            sl = slice(hh * DIL_HEAD_DIM, (hh + 1) * DIL_HEAD_DIM)
            o_nat[grp, hh, rows, :] = o_ref[0, r, :, sl].astype(F32)


def _dil_combine_head(hh, o_nat, l_nat):
    ls = [l_nat[grp, :, hh:hh + 1] for grp in range(DIL_GROUPS)]
    mx = jnp.maximum(jnp.maximum(ls[0], ls[1]), ls[2])
    es = [jnp.exp2(l - mx) for l in ls]
    den = es[0] + es[1] + es[2]
    return sum((e / den) * o_nat[grp, hh] for grp, e in enumerate(es))


def _gate2(z_half):
    return jnp.tanh(z_half) + 1.0


def _merge_kernel(x_ref, g_ref, up_ref, uc_ref, un_ref, cnt_ref, b_ref, o0_ref, o1_ref, o2_ref, l0_ref, l1_ref, l2_ref,
                  pw_ref, ps_ref, wg_ref, wa_ref, wb_ref, wc_ref, wo_ref, x2_ref, ext, o_nat, l_nat, *, tm, nt):
    x = x_ref[...]
    h = _rms(x, g_ref[...]).astype(BF16)
    iseq = pl.program_id(0) % nt
    hw = D_MODEL // 2
    cols = lambda hf: slice(hf * hw, (hf + 1) * hw)
    logits = lambda j, hf: _dot(h, wg_ref[:, j * D_MODEL + hf * hw:j * D_MODEL + (hf + 1) * hw])
    pool = lambda g: _pool_group(g, uc_ref, pw_ref, ps_ref, cnt_ref, ext, tm)
    o_refs, l_refs = (o0_ref, o1_ref, o2_ref), (l0_ref, l1_ref, l2_ref)
    b = b_ref[...]

    l1a = logits(1, 0)
    _pool_fill(up_ref, uc_ref, un_ref, ext, iseq, nt, tm)
    a0 = pool(0)
    tba = _dot(b, wb_ref[:, cols(0)])
    a1 = pool(1)
    l1b = logits(1, 1)
    a2 = pool(2)
    tbb = _dot(b, wb_ref[:, cols(1)])
    l0a = logits(0, 0)
    a3 = pool(3)
    a = jnp.concatenate([a0, a1, a2, a3], axis=1).astype(BF16)
    l0b = logits(0, 1)
    _dil_to_token_order(0, o_refs[0], l_refs[0], o_nat, l_nat, tm)
    _dil_to_token_order(1, o_refs[1], l_refs[1], o_nat, l_nat, tm)
    taa = _dot(a, wa_ref[:, cols(0)])
    _dil_to_token_order(2, o_refs[2], l_refs[2], o_nat, l_nat, tm)
    l2a = logits(2, 0)
    c01 = [_dil_combine_head(hh, o_nat, l_nat) for hh in (0, 1)]
    tab = _dot(a, wa_ref[:, cols(1)])
    c23 = [_dil_combine_head(hh, o_nat, l_nat) for hh in (2, 3)]
    c = jnp.concatenate(c01 + c23, axis=1).astype(BF16)
    l2b = logits(2, 1)
    gba = _gate2(l1a) * tba + _gate2(l0a) * taa
    tca = _dot(c, wc_ref[:, cols(0)])
    gbb = _gate2(l1b) * tbb + _gate2(l0b) * tab
    tcb = _dot(c, wc_ref[:, cols(1)])
    ma = (gba + _gate2(l2a) * tca).astype(BF16)
    acc = x + _dot(ma, wo_ref[cols(0), :])
    mb = (gbb + _gate2(l2b) * tcb).astype(BF16)
    x2_ref[...] = acc + _dot(mb, wo_ref[cols(1), :])


def _merge(x, u, cnt, b, os_, ls_, p, S, tm):
    T = x.shape[0]
    nt = S // tm
    hb = tm // POOL_HALO
    nhb = T // POOL_HALO
    row = lambda i: (i, 0)
    cls = lambda i: (i // nt, 0, i % nt, 0)
    br = pl.BlockSpec((tm, BRANCH_W), row)
    halo = lambda f: pl.BlockSpec((POOL_HALO, POOL_WIDTH), f)
    ds_ = [d for _, d in DIL_PAIRS]
    weights = [p[n] for n in ("pool_w", "pool_scale", "w_gate", "w_a", "w_b", "w_c", "w_o")]
    return pl.pallas_call(
        functools.partial(_merge_kernel, tm=tm, nt=nt),
        grid=(T // tm,),
        in_specs=[pl.BlockSpec((tm, D_MODEL), row), _resident((1, D_MODEL)),
                  halo(lambda i: (jnp.maximum(i * hb - 1, 0), 0)), br,
                  halo(lambda i: (jnp.minimum((i + 1) * hb, nhb - 1), 0)),
                  pl.BlockSpec((tm, 128), lambda i: (i % nt, 0)), br]
        + [pl.BlockSpec((1, d, tm // d, DIL_GROUP_W), cls) for d in ds_]
        + [pl.BlockSpec((1, d, tm // d, 128), cls) for d in ds_]
        + [_resident(w.shape) for w in weights],
        out_specs=pl.BlockSpec((tm, D_MODEL), row),
        out_shape=jax.ShapeDtypeStruct((T, D_MODEL), F32),
        scratch_shapes=[pltpu.VMEM((tm + 2 * POOL_HALO, POOL_WIDTH), F32),
                        pltpu.VMEM((DIL_GROUPS, DIL_HEADS, tm, DIL_HEAD_DIM), F32),
                        pltpu.VMEM((DIL_GROUPS, tm, 128), F32)],
        compiler_params=_params("parallel"),
        name="merge",
    )(x, p["ln1"], u, u, u, cnt, b, *os_, *ls_, *weights)


def _ffn_kernel(x_ref, g_ref, w1_ref, w2_ref, gf_ref, y_ref, *, final):
    x = x_ref[...]
    h = _rms(x, g_ref[...]).astype(BF16)
    acc = x
    for c in range(D_FF // FF_CHUNK):
        sl = slice(c * FF_CHUNK, (c + 1) * FF_CHUNK)
        f = jnp.maximum(_dot(h, w1_ref[:, sl]), 0.0)
        acc = acc + _dot((f * f).astype(BF16), w2_ref[sl, :])
    if final:
        acc = _rms(acc, gf_ref[...])
    y_ref[...] = acc


def _ffn(x, g, w1, w2, gf, final, tm):
    T = x.shape[0]
    row = lambda i: (i, 0)
    return pl.pallas_call(
        functools.partial(_ffn_kernel, final=final),
        grid=(T // tm,),
        in_specs=[pl.BlockSpec((tm, D_MODEL), row), _resident((1, D_MODEL)), _resident(w1.shape),
                  _resident(w2.shape), _resident((1, D_MODEL))],
        out_specs=pl.BlockSpec((tm, D_MODEL), row),
        out_shape=jax.ShapeDtypeStruct((T, D_MODEL), F32),
        compiler_params=_params("parallel"),
        name="ffn",
    )(x, g, w1, w2, gf)


def _rope_tables(S):
    pos = jnp.arange(S).astype(F32)[:, None]

    def cs(half):
        inv = ROPE_THETA ** (-jnp.arange(half, dtype=F32) / half)
        ang = pos * inv[None, :]
        return jnp.cos(ang), jnp.sin(ang)

    c, s = cs(DIL_HEAD_DIM // 2)
    dil = [c, c, -s, s]
    c, s = cs(MLA_ROPE // 2)
    pad = MLA_HEAD_PAD - MLA_QK
    one, zero, zpad = jnp.ones((S, MLA_NOPE), F32), jnp.zeros((S, MLA_NOPE), F32), jnp.zeros((S, pad), F32)
    return jnp.concatenate(dil + [one, c, c, zpad] + [zero, -s, s, zpad], axis=1)


def _pool_counts(S):
    t = jnp.arange(S)[:, None]
    half = jnp.asarray([w // 2 for w in POOL_WINDOWS] + [1] * (128 - len(POOL_WINDOWS)))[None, :]
    return (jnp.clip(t + half, 0, S) - jnp.clip(t - half, 0, S)).astype(F32)


def _layer_weights(l, ln1, w_in, pool_w, pool_scale, q_norm, kv_norm, w_uq, w_uk, w_uv,
                   w_a, w_b, w_c, w_o, ln2, w_ff1, w_ff2):
    o = IN_OFFSETS
    wi = w_in[l]
    half = MLA_ROPE // 2
    pad = MLA_HEAD_PAD - MLA_QK
    kr = wi[:, o[3]:o[4]]
    zl = jnp.zeros((D_MODEL, MLA_NOPE), F32)
    zr = jnp.zeros((D_MODEL, pad), F32)
    kr_placed = jnp.concatenate([zl, kr[:, :half], kr[:, half:], zr], axis=1)
    kr_swapped = jnp.concatenate([zl, kr[:, half:], kr[:, :half], zr], axis=1)
    w1 = jnp.concatenate([wi[:, o[0]:o[3]], kr_placed, kr_swapped], axis=1)
    uq = w_uq[l].reshape(MLA_Q_RANK, MLA_HEADS, MLA_QK)
    zq = jnp.zeros((MLA_Q_RANK, MLA_HEADS, pad), F32)
    wq = jnp.concatenate([uq, zq], axis=2)
    wqs = jnp.concatenate([jnp.zeros((MLA_Q_RANK, MLA_HEADS, MLA_NOPE), F32), uq[:, :, MLA_NOPE + half:],
                           uq[:, :, MLA_NOPE:MLA_NOPE + half], zq], axis=2)
    uk = w_uk[l].reshape(MLA_KV_RANK, MLA_HEADS, MLA_NOPE)
    wk = jnp.concatenate([uk, jnp.zeros((MLA_KV_RANK, MLA_HEADS, MLA_HEAD_PAD - MLA_NOPE), F32)], axis=2)
    uv = w_uv[l].reshape(MLA_KV_RANK, MLA_HEADS // 2, 2, MLA_V)
    zv = jnp.zeros((MLA_KV_RANK, MLA_HEADS // 2, MLA_V), F32)
    wv = jnp.stack([jnp.concatenate([uv[:, :, 0], zv], axis=2), jnp.concatenate([zv, uv[:, :, 1]], axis=2)], axis=2)
    hp = MLA_HEADS * MLA_HEAD_PAD
    b16 = lambda a: a.astype(BF16)
    return dict(
        ln1=ln1[l][None], ln2=ln2[l][None],
        w_dil=b16(wi[:, o[4]:o[7]]), w_gate=b16(0.5 * wi[:, o[7]:o[8]]), w1=b16(w1),
        q_norm=q_norm[l][None], kv_norm=kv_norm[l][None],
        wq=b16(wq.reshape(MLA_Q_RANK, hp)), wqs=b16(wqs.reshape(MLA_Q_RANK, hp)),
        wk=b16(wk.reshape(MLA_KV_RANK, hp)), wv=b16(wv.reshape(MLA_KV_RANK, hp)),
        pool_w=b16(pool_w[l]), pool_scale=pool_scale[l][None],
        w_a=b16(0.5 * w_a[l]), w_b=b16(0.5 * w_b[l]), w_c=b16(0.5 * w_c[l]), w_o=b16(w_o[l]),
        w_ff1=b16(w_ff1[l]), w_ff2=b16(w_ff2[l]),
    )


def _encoder_layer(x, B, S, p, tabs, cnt, gf, final):
    tm = min(512, S)
    *dil_qkv, u, qkv = _in_proj(x, p, tabs, B, S, tm)
    dil = [_dil_attn(dil_qkv[grp], grp, 2048) for grp in range(DIL_GROUPS)]
    b = _mla_attn(qkv, B, S, min(256, S), 4)
    x2 = _merge(x, u, cnt, b, [o for o, _ in dil], [l for _, l in dil], p, S, tm)
    return _ffn(x2, p["ln2"], p["w_ff1"], p["w_ff2"], gf, final, tm)


def kernel(x_prompt, x_sample, ln1, w_in, pool_w, pool_scale, q_norm, kv_norm, w_uq, w_uk, w_uv,
           w_a, w_b, w_c, w_o, ln2, w_ff1, w_ff2, final_norm):
    depth = w_in.shape[0]
    layers = [_layer_weights(l, ln1, w_in, pool_w, pool_scale, q_norm, kv_norm, w_uq, w_uk, w_uv,
                             w_a, w_b, w_c, w_o, ln2, w_ff1, w_ff2) for l in range(depth)]
    gf = final_norm[None]
    outs = []
    for x in (x_prompt, x_sample):
        B, S, D = x.shape
        tabs = _rope_tables(S)
        cnt = _pool_counts(S)
        h = x.reshape(B * S, D)
        for l in range(depth):
            h = _encoder_layer(h, B, S, layers[l], tabs, cnt, gf, l == depth - 1)
        outs.append(h.reshape(B, S, D))
    return tuple(outs)
```

```python
import functools

import numpy as np
import jax
import jax.numpy as jnp
from jax import lax
from jax.experimental import pallas as pl
from jax.experimental.pallas import tpu as pltpu

F32 = jnp.float32
BF16 = jnp.bfloat16

D_MODEL = 1024
RMS_EPS = 1e-6
ROPE_THETA = 10000.0
NEG_INF = -1e30
LOG2_E = 1.4426950408889634

POOL_WIDTH = 512
POOL_WINDOWS = (2, 4, 8, 16)
POOL_GW = POOL_WIDTH // len(POOL_WINDOWS)
POOL_HALO = 8

MLA_HEADS = 8
MLA_NOPE = 64
MLA_ROPE = 32
MLA_V = 64
MLA_QK = MLA_NOPE + MLA_ROPE
MLA_Q_RANK = 384
MLA_KV_RANK = 256
MLA_HEAD_PAD = 128

DIL_PAIRS = ((128, 1), (512, 4), (2048, 16))
DIL_GROUPS = len(DIL_PAIRS)
DIL_HEADS = 4
DIL_HEAD_DIM = 128
DIL_GROUP_W = DIL_HEADS * DIL_HEAD_DIM
DIL_QKV = DIL_GROUPS * DIL_GROUP_W
DIL_HALF_W = 64
DIL_QBLK = 128
DIL_AHEAD = 3
DIL_SPLIT = 4

N_BRANCH = 3
BRANCH_W = 512
D_FF = 4 * D_MODEL
FF_CHUNK = 1024

IN_SPLITS = (POOL_WIDTH, MLA_Q_RANK, MLA_KV_RANK, MLA_ROPE, DIL_QKV, DIL_QKV, DIL_QKV, N_BRANCH * D_MODEL)
IN_OFFSETS = [0] + [int(o) for o in np.cumsum(IN_SPLITS)]

VMEM_LIMIT = 56 * 1024 * 1024


def _params(*sem):
    return pltpu.CompilerParams(dimension_semantics=sem, vmem_limit_bytes=VMEM_LIMIT)


def _resident(shape):
    nd = len(shape)
    return pl.BlockSpec(shape, lambda *_: (0,) * nd, pipeline_mode=pl.Buffered(1))


def _rms(xf, g):
    return xf * lax.rsqrt(jnp.mean(xf * xf, axis=-1, keepdims=True) + RMS_EPS) * g


def _dot(a, b):
    return jnp.dot(a, b, preferred_element_type=F32)


def _dot_nt(a, b):
    return lax.dot_general(a, b, (((1,), (1,)), ((), ())), preferred_element_type=F32)


def _mla_proj(h, w1_ref, qn_ref, kvn_ref, wq_ref, wqs_ref, wk_ref, wv_ref, cos, sin, u_ref, qkv_ref):
    t = _dot(h, w1_ref[...])
    o_cq = POOL_WIDTH
    o_ckv = o_cq + MLA_Q_RANK
    o_kr = o_ckv + MLA_KV_RANK
    hp = MLA_HEADS * MLA_HEAD_PAD
    u_ref[...] = t[:, :o_cq]
    cqn = _rms(t[:, o_cq:o_ckv], qn_ref[...]).astype(BF16)
    cn = _rms(t[:, o_ckv:o_kr], kvn_ref[...]).astype(BF16)
    kr = t[:, o_kr:o_kr + MLA_HEAD_PAD] * cos + t[:, o_kr + MLA_HEAD_PAD:] * sin
    qf = _dot(cqn, wq_ref[...])
    qs = _dot(cqn, wqs_ref[...])
    kf = _dot(cn, wk_ref[...])
    scale = MLA_QK ** -0.5 * LOG2_E
    for hh in range(MLA_HEADS):
        sl = slice(hh * MLA_HEAD_PAD, (hh + 1) * MLA_HEAD_PAD)
        qkv_ref[:, sl] = ((qf[:, sl] * cos + qs[:, sl] * sin) * scale).astype(BF16)
        qkv_ref[:, hp + sl.start:hp + sl.stop] = (kf[:, sl] + kr).astype(BF16)
    lane = lax.broadcasted_iota(jnp.int32, (1, hp), 1)
    ones_lane = jnp.where((lane // MLA_HEAD_PAD) % 2 == 0, MLA_V, 0)
    ones = (lane % MLA_HEAD_PAD == ones_lane).astype(F32)
    qkv_ref[:, 2 * hp:] = (_dot(cn, wv_ref[...]) + ones).astype(BF16)


def _dil_proj(h, w_ref, cos, sin, out_refs, stage, mid, tm):
    scale = DIL_HEAD_DIM ** -0.5 * LOG2_E
    n = 0
    for grp in reversed(range(DIL_GROUPS)):
        d = DIL_PAIRS[grp][1]
        out = out_refs[grp]
        for which in range(3):
            c = which * DIL_GROUPS + grp
            acc = _dot(h, w_ref[:, c * DIL_GROUP_W:(c + 1) * DIL_GROUP_W])
            for hh in range(DIL_HEADS):
                sl = slice(hh * DIL_HEAD_DIM, (hh + 1) * DIL_HEAD_DIM)
                dst = slice(which * DIL_GROUP_W + sl.start, which * DIL_GROUP_W + sl.stop)
                blk = acc[:, sl]
                if which < 2:
                    blk = blk * cos + pltpu.roll(blk, DIL_HEAD_DIM // 2, axis=1) * sin
                if which == 0:
                    blk = blk * scale
                if d == 1:
                    out[0, 0, :, dst] = blk.astype(BF16)
                else:
                    stage[n % 2, hh] = blk
            if d > 1:
                for hh in range(DIL_HEADS):
                    sl = slice(hh * DIL_HEAD_DIM, (hh + 1) * DIL_HEAD_DIM)
                    dst = slice(which * DIL_GROUP_W + sl.start, which * DIL_GROUP_W + sl.stop)
                    if d == DIL_SPLIT * DIL_SPLIT:
                        for lo in range(DIL_SPLIT):
                            mid[n % 2, hh, lo] = stage[n % 2, hh, pl.ds(lo, tm // DIL_SPLIT, stride=DIL_SPLIT), :]
                        for r in range(d):
                            hi, lo = divmod(r, DIL_SPLIT)
                            rows = pl.ds(hi, tm // d, stride=DIL_SPLIT)
                            out[0, r, :, dst] = mid[n % 2, hh, lo, rows, :].astype(BF16)
                    else:
                        for r in range(d):
                            out[0, r, :, dst] = stage[n % 2, hh, pl.ds(r, tm // d, stride=d), :].astype(BF16)
                n += 1


def _in_proj_kernel(x_ref, g_ref, wd_ref, tab_ref, w1_ref, qn_ref, kvn_ref, wq_ref, wqs_ref, wk_ref, wv_ref,
                    *refs, tm):
    dil_out, (u_ref, qkv_ref), (stage, mid) = refs[:DIL_GROUPS], refs[DIL_GROUPS:DIL_GROUPS + 2], refs[-2:]
    h = _rms(x_ref[...], g_ref[...]).astype(BF16)
    tab = lambda n: tab_ref[:, n * 128:(n + 1) * 128]
    _mla_proj(h, w1_ref, qn_ref, kvn_ref, wq_ref, wqs_ref, wk_ref, wv_ref, tab(2), tab(3), u_ref, qkv_ref)
    _dil_proj(h, wd_ref, tab(0), tab(1), dil_out, stage, mid, tm)


def _in_proj(x, p, tabs, B, S, tm):
    T = x.shape[0]
    nt = S // tm
    row = lambda i: (i, 0)
    cls = lambda i: (i // nt, 0, i % nt, 0)
    hp = MLA_HEADS * MLA_HEAD_PAD
    ds_ = [d for _, d in DIL_PAIRS]
    weights = [p[n] for n in ("w1", "q_norm", "kv_norm", "wq", "wqs", "wk", "wv")]
    return pl.pallas_call(
        functools.partial(_in_proj_kernel, tm=tm),
        grid=(T // tm,),
        in_specs=[pl.BlockSpec((tm, D_MODEL), row), _resident((1, D_MODEL)), _resident(p["w_dil"].shape),
                  pl.BlockSpec((tm, tabs.shape[1]), lambda i: (i % nt, 0))]
        + [_resident(w.shape) for w in weights],
        out_specs=[pl.BlockSpec((1, d, tm // d, DIL_QKV), cls) for d in ds_]
        + [pl.BlockSpec((tm, POOL_WIDTH), row), pl.BlockSpec((tm, 3 * hp), row)],
        out_shape=[jax.ShapeDtypeStruct((B, d, S // d, DIL_QKV), BF16) for d in ds_]
        + [jax.ShapeDtypeStruct((T, POOL_WIDTH), F32), jax.ShapeDtypeStruct((T, 3 * hp), BF16)],
        scratch_shapes=[pltpu.VMEM((2, DIL_HEADS, tm, DIL_HEAD_DIM), F32),
                        pltpu.VMEM((2, DIL_HEADS, DIL_SPLIT, tm // DIL_SPLIT, DIL_HEAD_DIM), F32)],
        compiler_params=_params("parallel"),
        name="in_proj",
    )(x, p["ln1"], p["w_dil"], tabs, *weights)


def _dil_attn_kernel(q_ref, kp_ref, kc_ref, kn_ref, vp_ref, vc_ref, vn_ref, o_ref, lse_ref, kwin, vwin,
                     *, tl, L, dr):
    i = pl.program_id(2)
    hw = DIL_HALF_W
    for rr in range(dr):
        kwin[rr, 0:hw] = kp_ref[0, rr]
        kwin[rr, hw:hw + tl] = kc_ref[0, rr]
        kwin[rr, hw + tl:] = kn_ref[0, rr]
        vwin[rr, 0:hw] = vp_ref[0, rr]
        vwin[rr, hw:hw + tl] = vc_ref[0, rr]
        vwin[rr, hw + tl:] = vn_ref[0, rr]
    nk = DIL_QBLK + 2 * hw
    row = lax.broadcasted_iota(jnp.int32, (DIL_QBLK, nk), 0)
    col = lax.broadcasted_iota(jnp.int32, (DIL_QBLK, nk), 1)
    band = (col >= row) & (col - row <= 2 * hw)
    lane = lax.broadcasted_iota(jnp.int32, (DIL_QBLK, 128), 1)
    head = lambda hh: slice(hh * DIL_HEAD_DIM, (hh + 1) * DIL_HEAD_DIM)
    units = [(j, rr, hh) for j in range(tl // DIL_QBLK) for rr in range(dr) for hh in range(DIL_HEADS)]

    def scores(u):
        j, rr, hh = u
        r0 = j * DIL_QBLK
        return _dot_nt(q_ref[0, rr, r0:r0 + DIL_QBLK, head(hh)], kwin[rr, r0:r0 + nk, head(hh)])

    ahead = [scores(u) for u in units[:DIL_AHEAD]]
    valid = lse_tile = None
    for n, (j, rr, hh) in enumerate(units):
        r0 = j * DIL_QBLK
        s = ahead.pop(0)
        if n + DIL_AHEAD < len(units):
            ahead.append(scores(units[n + DIL_AHEAD]))
        if rr == 0 and hh == 0:
            kidx = i * tl + r0 - hw + col
            valid = band & (kidx >= 0) & (kidx < L)
        if hh == 0:
            lse_tile = jnp.zeros((DIL_QBLK, 128), F32)
        s = jnp.where(valid, s, NEG_INF)
        m = jnp.max(s, axis=-1, keepdims=True)
        p = jnp.exp2(s - m)
        l = jnp.sum(p, axis=-1, keepdims=True)
        o = _dot(p.astype(BF16), vwin[rr, r0:r0 + nk, head(hh)]) / l
        o_ref[0, rr, r0:r0 + DIL_QBLK, head(hh)] = o.astype(BF16)
        lse_tile = jnp.where(lane == hh, m + jnp.log2(l), lse_tile)
        if hh == DIL_HEADS - 1:
            lse_ref[0, rr, r0:r0 + DIL_QBLK, :] = lse_tile


def _dil_attn(qkv, grp, rows):
    B, d, L, _ = qkv.shape
    tl = min(rows, L)
    dr = min(d, max(1, rows // tl))
    hb = tl // DIL_HALF_W
    nhb = L // DIL_HALF_W
    cur = lambda w: lambda b, r, i: (b, r, i, w)
    prev = lambda w: lambda b, r, i: (b, r, jnp.maximum(i * hb - 1, 0), w)
    nxt = lambda w: lambda b, r, i: (b, r, jnp.minimum((i + 1) * hb, nhb - 1), w)
    blk = lambda n, f: pl.BlockSpec((1, dr, n, DIL_GROUP_W), f)
    return pl.pallas_call(
        functools.partial(_dil_attn_kernel, tl=tl, L=L, dr=dr),
        grid=(B, d // dr, L // tl),
        in_specs=[blk(tl, cur(0)), blk(DIL_HALF_W, prev(1)), blk(tl, cur(1)), blk(DIL_HALF_W, nxt(1)),
                  blk(DIL_HALF_W, prev(2)), blk(tl, cur(2)), blk(DIL_HALF_W, nxt(2))],
        out_specs=[pl.BlockSpec((1, dr, tl, DIL_GROUP_W), cur(0)), pl.BlockSpec((1, dr, tl, 128), cur(0))],
        out_shape=[jax.ShapeDtypeStruct((B, d, L, DIL_GROUP_W), BF16),
                   jax.ShapeDtypeStruct((B, d, L, 128), F32)],
        scratch_shapes=[pltpu.VMEM((dr, tl + 2 * DIL_HALF_W, DIL_GROUP_W), BF16)] * 2,
        compiler_params=_params("parallel", "parallel", "parallel"),
        name=f"dil_attn_g{grp}",
    )(*([qkv] * 7))


MLA_HEADS_PER_STEP = 4


def _mla_attn_kernel(q_ref, k_ref, v_ref, o_ref, *, tq, unroll):
    lane = lax.broadcasted_iota(jnp.int32, (tq, 2 * MLA_V), 1)
    head = lambda h: slice(h * MLA_HEAD_PAD, (h + 1) * MLA_HEAD_PAD)
    units = [(u, h) for u in range(unroll) for h in range(MLA_HEADS_PER_STEP)]

    def trip(t, carry):
        rows = [pl.ds(pl.multiple_of((t * unroll + u) * tq, tq), tq) for u in range(unroll)]
        scores = lambda un: _dot_nt(q_ref[rows[un[0]], head(un[1])], k_ref[:, head(un[1])])
        outs = {}
        s_next = scores(units[0])
        for n, (u, h) in enumerate(units):
            s = s_next
            if n + 1 < len(units):
                s_next = scores(units[n + 1])
            m = jnp.max(s, axis=-1, keepdims=True)
            p = jnp.exp2(s - m).astype(BF16)
            ov = _dot(p, v_ref[:, head(h)])
            lcol = MLA_V if h % 2 == 0 else 0
            outs[h] = ov / ov[:, lcol:lcol + 1]
            if h % 2 == 1:
                pair = jnp.where(lane < MLA_V, outs[h - 1], outs[h])
                o_ref[rows[u], (h // 2) * 2 * MLA_V:(h // 2 + 1) * 2 * MLA_V] = pair.astype(BF16)
        return carry

    lax.fori_loop(0, q_ref.shape[0] // (tq * unroll), trip, 0)


def _mla_attn(qkv, B, S, tq, tiles):
    T = B * S
    bq = tq * tiles
    nq = S // bq
    nh = MLA_HEADS_PER_STEP
    ng = MLA_HEADS // nh
    kv_bytes = 2 * S * nh * MLA_HEAD_PAD * 2
    roomy = 2 * kv_bytes <= VMEM_LIMIT // 2
    mode = {} if roomy else dict(pipeline_mode=pl.Buffered(1))
    kv = lambda which: pl.BlockSpec((S, nh * MLA_HEAD_PAD), lambda b, p, i: (b, which * ng + p), **mode)
    return pl.pallas_call(
        functools.partial(_mla_attn_kernel, tq=tq, unroll=2 if roomy else 1),
        grid=(B, ng, nq),
        in_specs=[pl.BlockSpec((bq, nh * MLA_HEAD_PAD), lambda b, p, i: (b * nq + i, p)), kv(1), kv(2)],
        out_specs=pl.BlockSpec((bq, nh * MLA_V), lambda b, p, i: (b * nq + i, p)),
        out_shape=jax.ShapeDtypeStruct((T, MLA_HEADS * MLA_V), BF16),
        compiler_params=_params("parallel", "parallel", "parallel"),
        name="mla_attn",
    )(qkv, qkv, qkv)


def _pool_fill(up_ref, uc_ref, un_ref, ext, iseq, nt, tm):
    hl = POOL_HALO
    ext[0:hl] = jnp.where(iseq > 0, up_ref[...], 0.0)
    ext[hl:hl + tm] = uc_ref[...]
    ext[hl + tm:] = jnp.where(iseq < nt - 1, un_ref[...], 0.0)


def _pool_group(g, uc_ref, w_ref, sc_ref, cnt_ref, ext, tm):
    hl, w = POOL_HALO, POOL_WINDOWS[g]
    cols = slice(g * POOL_GW, (g + 1) * POOL_GW)
    acc = ext[hl - w // 2:hl - w // 2 + tm, cols]
    for off in range(-w // 2 + 1, w // 2):
        acc = acc + ext[hl + off:hl + off + tm, cols]
    mixed = acc / cnt_ref[:, g:g + 1] - uc_ref[:, cols]
    return _dot(mixed.astype(BF16), w_ref[g]) * sc_ref[:, cols]


def _dil_to_token_order(grp, o_ref, l_ref, o_nat, l_nat, tm):
    d = DIL_PAIRS[grp][1]
    for r in range(d):
        rows = pl.ds(r, tm // d, stride=d) if d > 1 else slice(None)
        l_nat[grp, rows, :] = l_ref[0, r]
        for hh in range(DIL_HEADS):
            sl = slice(hh * DIL_HEAD_DIM, (hh + 1) * DIL_HEAD_DIM)
            o_nat[grp, hh, rows, :] = o_ref[0, r, :, sl].astype(F32)


def _dil_combine_head(hh, o_nat, l_nat):
    ls = [l_nat[grp, :, hh:hh + 1] for grp in range(DIL_GROUPS)]
    mx = jnp.maximum(jnp.maximum(ls[0], ls[1]), ls[2])
    es = [jnp.exp2(l - mx) for l in ls]
    den = es[0] + es[1] + es[2]
    return sum((e / den) * o_nat[grp, hh] for grp, e in enumerate(es))


def _gate2(z_half):
    return jnp.tanh(z_half) + 1.0


def _merge_kernel(x_ref, g_ref, up_ref, uc_ref, un_ref, cnt_ref, b_ref, o0_ref, o1_ref, o2_ref, l0_ref, l1_ref, l2_ref,
                  pw_ref, ps_ref, wg_ref, wa_ref, wb_ref, wc_ref, wo_ref, x2_ref, ext, o_nat, l_nat, *, tm, nt):
    x = x_ref[...]
    h = _rms(x, g_ref[...]).astype(BF16)
    iseq = pl.program_id(0) % nt
    hw = D_MODEL // 2
    cols = lambda hf: slice(hf * hw, (hf + 1) * hw)
    logits = lambda j, hf: _dot(h, wg_ref[:, j * D_MODEL + hf * hw:j * D_MODEL + (hf + 1) * hw])
    pool = lambda g: _pool_group(g, uc_ref, pw_ref, ps_ref, cnt_ref, ext, tm)
    o_refs, l_refs = (o0_ref, o1_ref, o2_ref), (l0_ref, l1_ref, l2_ref)
    b = b_ref[...]

    l1a = logits(1, 0)
    _pool_fill(up_ref, uc_ref, un_ref, ext, iseq, nt, tm)
    a0 = pool(0)
    tba = _dot(b, wb_ref[:, cols(0)])
    a1 = pool(1)
    l1b = logits(1, 1)
    a2 = pool(2)
    tbb = _dot(b, wb_ref[:, cols(1)])
    l0a = logits(0, 0)
    a3 = pool(3)
    a = jnp.concatenate([a0, a1, a2, a3], axis=1).astype(BF16)
    l0b = logits(0, 1)
    _dil_to_token_order(0, o_refs[0], l_refs[0], o_nat, l_nat, tm)
    _dil_to_token_order(1, o_refs[1], l_refs[1], o_nat, l_nat, tm)
    taa = _dot(a, wa_ref[:, cols(0)])
    _dil_to_token_order(2, o_refs[2], l_refs[2], o_nat, l_nat, tm)
    l2a = logits(2, 0)
    c01 = [_dil_combine_head(hh, o_nat, l_nat) for hh in (0, 1)]
    tab = _dot(a, wa_ref[:, cols(1)])
    c23 = [_dil_combine_head(hh, o_nat, l_nat) for hh in (2, 3)]
    c = jnp.concatenate(c01 + c23, axis=1).astype(BF16)
    l2b = logits(2, 1)
    gba = _gate2(l1a) * tba + _gate2(l0a) * taa
    tca = _dot(c, wc_ref[:, cols(0)])
    gbb = _gate2(l1b) * tbb + _gate2(l0b) * tab
    tcb = _dot(c, wc_ref[:, cols(1)])
    ma = (gba + _gate2(l2a) * tca).astype(BF16)
    acc = x + _dot(ma, wo_ref[cols(0), :])
    mb = (gbb + _gate2(l2b) * tcb).astype(BF16)
    x2_ref[...] = acc + _dot(mb, wo_ref[cols(1), :])


def _merge(x, u, cnt, b, os_, ls_, p, S, tm):
    T = x.shape[0]
    nt = S // tm
    hb = tm // POOL_HALO
    nhb = T // POOL_HALO
    row = lambda i: (i, 0)
    cls = lambda i: (i // nt, 0, i % nt, 0)
    br = pl.BlockSpec((tm, BRANCH_W), row)
    halo = lambda f: pl.BlockSpec((POOL_HALO, POOL_WIDTH), f)
    ds_ = [d for _, d in DIL_PAIRS]
    weights = [p[n] for n in ("pool_w", "pool_scale", "w_gate", "w_a", "w_b", "w_c", "w_o")]
    return pl.pallas_call(
        functools.partial(_merge_kernel, tm=tm, nt=nt),
        grid=(T // tm,),
        in_specs=[pl.BlockSpec((tm, D_MODEL), row), _resident((1, D_MODEL)),
                  halo(lambda i: (jnp.maximum(i * hb - 1, 0), 0)), br,
                  halo(lambda i: (jnp.minimum((i + 1) * hb, nhb - 1), 0)),
                  pl.BlockSpec((tm, 128), lambda i: (i % nt, 0)), br]
        + [pl.BlockSpec((1, d, tm // d, DIL_GROUP_W), cls) for d in ds_]
        + [pl.BlockSpec((1, d, tm // d, 128), cls) for d in ds_]
        + [_resident(w.shape) for w in weights],
        out_specs=pl.BlockSpec((tm, D_MODEL), row),
        out_shape=jax.ShapeDtypeStruct((T, D_MODEL), F32),
        scratch_shapes=[pltpu.VMEM((tm + 2 * POOL_HALO, POOL_WIDTH), F32),
                        pltpu.VMEM((DIL_GROUPS, DIL_HEADS, tm, DIL_HEAD_DIM), F32),
                        pltpu.VMEM((DIL_GROUPS, tm, 128), F32)],
        compiler_params=_params("parallel"),
        name="merge",
    )(x, p["ln1"], u, u, u, cnt, b, *os_, *ls_, *weights)


def _ffn_kernel(x_ref, g_ref, w1_ref, w2_ref, gf_ref, y_ref, *, final):
    x = x_ref[...]
    h = _rms(x, g_ref[...]).astype(BF16)
    acc = x
    for c in range(D_FF // FF_CHUNK):
        sl = slice(c * FF_CHUNK, (c + 1) * FF_CHUNK)
        f = jnp.maximum(_dot(h, w1_ref[:, sl]), 0.0)
        acc = acc + _dot((f * f).astype(BF16), w2_ref[sl, :])
    if final:
        acc = _rms(acc, gf_ref[...])
    y_ref[...] = acc


def _ffn(x, g, w1, w2, gf, final, tm):
    T = x.shape[0]
    row = lambda i: (i, 0)
    return pl.pallas_call(
        functools.partial(_ffn_kernel, final=final),
        grid=(T // tm,),
        in_specs=[pl.BlockSpec((tm, D_MODEL), row), _resident((1, D_MODEL)), _resident(w1.shape),
                  _resident(w2.shape), _resident((1, D_MODEL))],
        out_specs=pl.BlockSpec((tm, D_MODEL), row),
        out_shape=jax.ShapeDtypeStruct((T, D_MODEL), F32),
        compiler_params=_params("parallel"),
        name="ffn",
    )(x, g, w1, w2, gf)


def _rope_tables(S):
    pos = jnp.arange(S).astype(F32)[:, None]

    def cs(half):
        inv = ROPE_THETA ** (-jnp.arange(half, dtype=F32) / half)
        ang = pos * inv[None, :]
        return jnp.cos(ang), jnp.sin(ang)

    c, s = cs(DIL_HEAD_DIM // 2)
    dil = [c, c, -s, s]
    c, s = cs(MLA_ROPE // 2)
    pad = MLA_HEAD_PAD - MLA_QK
    one, zero, zpad = jnp.ones((S, MLA_NOPE), F32), jnp.zeros((S, MLA_NOPE), F32), jnp.zeros((S, pad), F32)
    return jnp.concatenate(dil + [one, c, c, zpad] + [zero, -s, s, zpad], axis=1)


def _pool_counts(S):
    t = jnp.arange(S)[:, None]
    half = jnp.asarray([w // 2 for w in POOL_WINDOWS] + [1] * (128 - len(POOL_WINDOWS)))[None, :]
    return (jnp.clip(t + half, 0, S) - jnp.clip(t - half, 0, S)).astype(F32)


def _layer_weights(l, ln1, w_in, pool_w, pool_scale, q_norm, kv_norm, w_uq, w_uk, w_uv,
                   w_a, w_b, w_c, w_o, ln2, w_ff1, w_ff2):
    o = IN_OFFSETS
    wi = w_in[l]
    half = MLA_ROPE // 2
    pad = MLA_HEAD_PAD - MLA_QK
    kr = wi[:, o[3]:o[4]]
    zl = jnp.zeros((D_MODEL, MLA_NOPE), F32)
    zr = jnp.zeros((D_MODEL, pad), F32)
    kr_placed = jnp.concatenate([zl, kr[:, :half], kr[:, half:], zr], axis=1)
    kr_swapped = jnp.concatenate([zl, kr[:, half:], kr[:, :half], zr], axis=1)
    w1 = jnp.concatenate([wi[:, o[0]:o[3]], kr_placed, kr_swapped], axis=1)
    uq = w_uq[l].reshape(MLA_Q_RANK, MLA_HEADS, MLA_QK)
    zq = jnp.zeros((MLA_Q_RANK, MLA_HEADS, pad), F32)
    wq = jnp.concatenate([uq, zq], axis=2)
    wqs = jnp.concatenate([jnp.zeros((MLA_Q_RANK, MLA_HEADS, MLA_NOPE), F32), uq[:, :, MLA_NOPE + half:],
                           uq[:, :, MLA_NOPE:MLA_NOPE + half], zq], axis=2)
    uk = w_uk[l].reshape(MLA_KV_RANK, MLA_HEADS, MLA_NOPE)
    wk = jnp.concatenate([uk, jnp.zeros((MLA_KV_RANK, MLA_HEADS, MLA_HEAD_PAD - MLA_NOPE), F32)], axis=2)
    uv = w_uv[l].reshape(MLA_KV_RANK, MLA_HEADS // 2, 2, MLA_V)
    zv = jnp.zeros((MLA_KV_RANK, MLA_HEADS // 2, MLA_V), F32)
    wv = jnp.stack([jnp.concatenate([uv[:, :, 0], zv], axis=2), jnp.concatenate([zv, uv[:, :, 1]], axis=2)], axis=2)
    hp = MLA_HEADS * MLA_HEAD_PAD
    b16 = lambda a: a.astype(BF16)
    return dict(
        ln1=ln1[l][None], ln2=ln2[l][None],
        w_dil=b16(wi[:, o[4]:o[7]]), w_gate=b16(0.5 * wi[:, o[7]:o[8]]), w1=b16(w1),
        q_norm=q_norm[l][None], kv_norm=kv_norm[l][None],
        wq=b16(wq.reshape(MLA_Q_RANK, hp)), wqs=b16(wqs.reshape(MLA_Q_RANK, hp)),
        wk=b16(wk.reshape(MLA_KV_RANK, hp)), wv=b16(wv.reshape(MLA_KV_RANK, hp)),
        pool_w=b16(pool_w[l]), pool_scale=pool_scale[l][None],
        w_a=b16(0.5 * w_a[l]), w_b=b16(0.5 * w_b[l]), w_c=b16(0.5 * w_c[l]), w_o=b16(w_o[l]),
        w_ff1=b16(w_ff1[l]), w_ff2=b16(w_ff2[l]),
    )


def _encoder_layer(x, B, S, p, tabs, cnt, gf, final):
    tm = min(512, S)
    *dil_qkv, u, qkv = _in_proj(x, p, tabs, B, S, tm)
    dil = [_dil_attn(dil_qkv[grp], grp, 2048) for grp in range(DIL_GROUPS)]
    b = _mla_attn(qkv, B, S, min(256, S), 4)
    x2 = _merge(x, u, cnt, b, [o for o, _ in dil], [l for _, l in dil], p, S, tm)
    return _ffn(x2, p["ln2"], p["w_ff1"], p["w_ff2"], gf, final, tm)


def kernel(x_prompt, x_sample, ln1, w_in, pool_w, pool_scale, q_norm, kv_norm, w_uq, w_uk, w_uv,
           w_a, w_b, w_c, w_o, ln2, w_ff1, w_ff2, final_norm):
    depth = w_in.shape[0]
    layers = [_layer_weights(l, ln1, w_in, pool_w, pool_scale, q_norm, kv_norm, w_uq, w_uk, w_uv,
                             w_a, w_b, w_c, w_o, ln2, w_ff1, w_ff2) for l in range(depth)]
    gf = final_norm[None]
    outs = []
    for x in (x_prompt, x_sample):
        B, S, D = x.shape
        tabs = _rope_tables(S)
        cnt = _pool_counts(S)
        h = x.reshape(B * S, D)
        for l in range(depth):
            h = _encoder_layer(h, B, S, layers[l], tabs, cnt, gf, l == depth - 1)
        outs.append(h.reshape(B, S, D))
    return tuple(outs)
```

```python
import functools

import numpy as np
import jax
import jax.numpy as jnp
from jax import lax
from jax.experimental import pallas as pl
from jax.experimental.pallas import tpu as pltpu

F32 = jnp.float32
BF16 = jnp.bfloat16

D_MODEL = 1024
RMS_EPS = 1e-6
ROPE_THETA = 10000.0
NEG_INF = -1e30
LOG2_E = 1.4426950408889634

POOL_WIDTH = 512
POOL_WINDOWS = (2, 4, 8, 16)
POOL_GW = POOL_WIDTH // len(POOL_WINDOWS)
POOL_HALO = 8

MLA_HEADS = 8
MLA_NOPE = 64
MLA_ROPE = 32
MLA_V = 64
MLA_QK = MLA_NOPE + MLA_ROPE
MLA_Q_RANK = 384
MLA_KV_RANK = 256
MLA_HEAD_PAD = 128

DIL_PAIRS = ((128, 1), (512, 4), (2048, 16))
DIL_GROUPS = len(DIL_PAIRS)
DIL_HEADS = 4
DIL_HEAD_DIM = 128
DIL_GROUP_W = DIL_HEADS * DIL_HEAD_DIM
DIL_QKV = DIL_GROUPS * DIL_GROUP_W
DIL_HALF_W = 64
DIL_QBLK = 128
DIL_AHEAD = 3
DIL_SPLIT = 4

N_BRANCH = 3
BRANCH_W = 512
D_FF = 4 * D_MODEL
FF_CHUNK = 1024

IN_SPLITS = (POOL_WIDTH, MLA_Q_RANK, MLA_KV_RANK, MLA_ROPE, DIL_QKV, DIL_QKV, DIL_QKV, N_BRANCH * D_MODEL)
IN_OFFSETS = [0] + [int(o) for o in np.cumsum(IN_SPLITS)]

VMEM_LIMIT = 56 * 1024 * 1024


def _params(*sem):
    return pltpu.CompilerParams(dimension_semantics=sem, vmem_limit_bytes=VMEM_LIMIT)


def _resident(shape):
    nd = len(shape)
    return pl.BlockSpec(shape, lambda *_: (0,) * nd, pipeline_mode=pl.Buffered(1))


def _rms(xf, g):
    return xf * lax.rsqrt(jnp.mean(xf * xf, axis=-1, keepdims=True) + RMS_EPS) * g


def _dot(a, b):
    return jnp.dot(a, b, preferred_element_type=F32)


def _dot_nt(a, b):
    return lax.dot_general(a, b, (((1,), (1,)), ((), ())), preferred_element_type=F32)


def _mla_proj(h, w1_ref, qn_ref, kvn_ref, wq_ref, wk_ref, wv_ref, cos, sin_lo, sin_hi, u_ref, qkv_ref):
    t = _dot(h, w1_ref[...])
    o_cq = POOL_WIDTH
    o_ckv = o_cq + MLA_Q_RANK
    o_kr = o_ckv + MLA_KV_RANK
    hp = MLA_HEADS * MLA_HEAD_PAD
    half = MLA_ROPE // 2

    def rope(z):
        return (z * cos + pltpu.roll(z, MLA_HEAD_PAD - half, axis=1) * sin_lo
                + pltpu.roll(z, half, axis=1) * sin_hi)

    u_ref[...] = t[:, :o_cq]
    cqn = _rms(t[:, o_cq:o_ckv], qn_ref[...]).astype(BF16)
    cn = _rms(t[:, o_ckv:o_kr], kvn_ref[...]).astype(BF16)
    kr = rope(t[:, o_kr:])
    qf = _dot(cqn, wq_ref[...])
    kf = _dot(cn, wk_ref[...])
    scale = MLA_QK ** -0.5 * LOG2_E
    for hh in range(MLA_HEADS):
        sl = slice(hh * MLA_HEAD_PAD, (hh + 1) * MLA_HEAD_PAD)
        qkv_ref[:, sl] = (rope(qf[:, sl]) * scale).astype(BF16)
        qkv_ref[:, hp + sl.start:hp + sl.stop] = (kf[:, sl] + kr).astype(BF16)
    lane = lax.broadcasted_iota(jnp.int32, (1, hp), 1)
    ones_lane = jnp.where((lane // MLA_HEAD_PAD) % 2 == 0, MLA_V, 0)
    ones = (lane % MLA_HEAD_PAD == ones_lane).astype(F32)
    qkv_ref[:, 2 * hp:] = (_dot(cn, wv_ref[...]) + ones).astype(BF16)


def _dil_proj(h, w_ref, cos, sin, out_refs, stage, mid, tm):
    scale = DIL_HEAD_DIM ** -0.5 * LOG2_E
    n = 0
    for grp in reversed(range(DIL_GROUPS)):
        d = DIL_PAIRS[grp][1]
        out = out_refs[grp]
        for which in range(3):
            c = which * DIL_GROUPS + grp
            acc = _dot(h, w_ref[:, c * DIL_GROUP_W:(c + 1) * DIL_GROUP_W])
            for hh in range(DIL_HEADS):
                sl = slice(hh * DIL_HEAD_DIM, (hh + 1) * DIL_HEAD_DIM)
                dst = slice(which * DIL_GROUP_W + sl.start, which * DIL_GROUP_W + sl.stop)
                blk = acc[:, sl]
                if which < 2:
                    blk = blk * cos + pltpu.roll(blk, DIL_HEAD_DIM // 2, axis=1) * sin
                if which == 0:
                    blk = blk * scale
                if d == 1:
                    out[0, 0, :, dst] = blk.astype(BF16)
                else:
                    stage[n % 2, hh] = blk
            if d > 1:
                for hh in range(DIL_HEADS):
                    sl = slice(hh * DIL_HEAD_DIM, (hh + 1) * DIL_HEAD_DIM)
                    dst = slice(which * DIL_GROUP_W + sl.start, which * DIL_GROUP_W + sl.stop)
                    if d == DIL_SPLIT * DIL_SPLIT:
                        for lo in range(DIL_SPLIT):
                            mid[n % 2, hh, lo] = stage[n % 2, hh, pl.ds(lo, tm // DIL_SPLIT, stride=DIL_SPLIT), :]
                        for r in range(d):
                            hi, lo = divmod(r, DIL_SPLIT)
                            rows = pl.ds(hi, tm // d, stride=DIL_SPLIT)
                            out[0, r, :, dst] = mid[n % 2, hh, lo, rows, :].astype(BF16)
                    else:
                        for r in range(d):
                            out[0, r, :, dst] = stage[n % 2, hh, pl.ds(r, tm // d, stride=d), :].astype(BF16)
                n += 1


def _in_proj_kernel(x_ref, g_ref, wd_ref, tab_ref, w1_ref, qn_ref, kvn_ref, wq_ref, wk_ref, wv_ref, *refs, tm):
    dil_out, (u_ref, qkv_ref), (stage, mid) = refs[:DIL_GROUPS], refs[DIL_GROUPS:DIL_GROUPS + 2], refs[-2:]
    h = _rms(x_ref[...], g_ref[...]).astype(BF16)
    tab = lambda n: tab_ref[:, n * 128:(n + 1) * 128]
    _mla_proj(h, w1_ref, qn_ref, kvn_ref, wq_ref, wk_ref, wv_ref, tab(2), tab(3), tab(4), u_ref, qkv_ref)
    _dil_proj(h, wd_ref, tab(0), tab(1), dil_out, stage, mid, tm)


def _in_proj(x, p, tabs, B, S, tm):
    T = x.shape[0]
    nt = S // tm
    row = lambda i: (i, 0)
    cls = lambda i: (i // nt, 0, i % nt, 0)
    hp = MLA_HEADS * MLA_HEAD_PAD
    ds_ = [d for _, d in DIL_PAIRS]
    weights = [p[n] for n in ("w1", "q_norm", "kv_norm", "wq", "wk", "wv")]
    return pl.pallas_call(
        functools.partial(_in_proj_kernel, tm=tm),
        grid=(T // tm,),
        in_specs=[pl.BlockSpec((tm, D_MODEL), row), _resident((1, D_MODEL)), _resident(p["w_dil"].shape),
                  pl.BlockSpec((tm, tabs.shape[1]), lambda i: (i % nt, 0))]
        + [_resident(w.shape) for w in weights],
        out_specs=[pl.BlockSpec((1, d, tm // d, DIL_QKV), cls) for d in ds_]
        + [pl.BlockSpec((tm, POOL_WIDTH), row), pl.BlockSpec((tm, 3 * hp), row)],
        out_shape=[jax.ShapeDtypeStruct((B, d, S // d, DIL_QKV), BF16) for d in ds_]
        + [jax.ShapeDtypeStruct((T, POOL_WIDTH), F32), jax.ShapeDtypeStruct((T, 3 * hp), BF16)],
        scratch_shapes=[pltpu.VMEM((2, DIL_HEADS, tm, DIL_HEAD_DIM), F32),
                        pltpu.VMEM((2, DIL_HEADS, DIL_SPLIT, tm // DIL_SPLIT, DIL_HEAD_DIM), F32)],
        compiler_params=_params("parallel"),
        name="in_proj",
    )(x, p["ln1"], p["w_dil"], tabs, *weights)


def _dil_attn_kernel(q_ref, kp_ref, kc_ref, kn_ref, vp_ref, vc_ref, vn_ref, o_ref, lse_ref, kwin, vwin,
                     *, tl, L, dr):
    i = pl.program_id(2)
    hw = DIL_HALF_W
    for rr in range(dr):
        kwin[rr, 0:hw] = kp_ref[0, rr]
        kwin[rr, hw:hw + tl] = kc_ref[0, rr]
        kwin[rr, hw + tl:] = kn_ref[0, rr]
        vwin[rr, 0:hw] = vp_ref[0, rr]
        vwin[rr, hw:hw + tl] = vc_ref[0, rr]
        vwin[rr, hw + tl:] = vn_ref[0, rr]
    nk = DIL_QBLK + 2 * hw
    row = lax.broadcasted_iota(jnp.int32, (DIL_QBLK, nk), 0)
    col = lax.broadcasted_iota(jnp.int32, (DIL_QBLK, nk), 1)
    band = (col >= row) & (col - row <= 2 * hw)
    lane = lax.broadcasted_iota(jnp.int32, (DIL_QBLK, 128), 1)
    head = lambda hh: slice(hh * DIL_HEAD_DIM, (hh + 1) * DIL_HEAD_DIM)
    units = [(j, rr, hh) for j in range(tl // DIL_QBLK) for rr in range(dr) for hh in range(DIL_HEADS)]

    def scores(u):
        j, rr, hh = u
        r0 = j * DIL_QBLK
        return _dot_nt(q_ref[0, rr, r0:r0 + DIL_QBLK, head(hh)], kwin[rr, r0:r0 + nk, head(hh)])

    ahead = [scores(u) for u in units[:DIL_AHEAD]]
    valid = lse_tile = None
    for n, (j, rr, hh) in enumerate(units):
        r0 = j * DIL_QBLK
        s = ahead.pop(0)
        if n + DIL_AHEAD < len(units):
            ahead.append(scores(units[n + DIL_AHEAD]))
        if rr == 0 and hh == 0:
            kidx = i * tl + r0 - hw + col
            valid = band & (kidx >= 0) & (kidx < L)
        if hh == 0:
            lse_tile = jnp.zeros((DIL_QBLK, 128), F32)
        s = jnp.where(valid, s, NEG_INF)
        m = jnp.max(s, axis=-1, keepdims=True)
        p = jnp.exp2(s - m)
        l = jnp.sum(p, axis=-1, keepdims=True)
        o = _dot(p.astype(BF16), vwin[rr, r0:r0 + nk, head(hh)]) / l
        o_ref[0, rr, r0:r0 + DIL_QBLK, head(hh)] = o.astype(BF16)
        lse_tile = jnp.where(lane == hh, m + jnp.log2(l), lse_tile)
        if hh == DIL_HEADS - 1:
            lse_ref[0, rr, r0:r0 + DIL_QBLK, :] = lse_tile


def _dil_attn(qkv, grp, rows):
    B, d, L, _ = qkv.shape
    tl = min(rows, L)
    dr = min(d, max(1, rows // tl))
    hb = tl // DIL_HALF_W
    nhb = L // DIL_HALF_W
    cur = lambda w: lambda b, r, i: (b, r, i, w)
    prev = lambda w: lambda b, r, i: (b, r, jnp.maximum(i * hb - 1, 0), w)
    nxt = lambda w: lambda b, r, i: (b, r, jnp.minimum((i + 1) * hb, nhb - 1), w)
    blk = lambda n, f: pl.BlockSpec((1, dr, n, DIL_GROUP_W), f)
    return pl.pallas_call(
        functools.partial(_dil_attn_kernel, tl=tl, L=L, dr=dr),
        grid=(B, d // dr, L // tl),
        in_specs=[blk(tl, cur(0)), blk(DIL_HALF_W, prev(1)), blk(tl, cur(1)), blk(DIL_HALF_W, nxt(1)),
                  blk(DIL_HALF_W, prev(2)), blk(tl, cur(2)), blk(DIL_HALF_W, nxt(2))],
        out_specs=[pl.BlockSpec((1, dr, tl, DIL_GROUP_W), cur(0)), pl.BlockSpec((1, dr, tl, 128), cur(0))],
        out_shape=[jax.ShapeDtypeStruct((B, d, L, DIL_GROUP_W), BF16),
                   jax.ShapeDtypeStruct((B, d, L, 128), F32)],
        scratch_shapes=[pltpu.VMEM((dr, tl + 2 * DIL_HALF_W, DIL_GROUP_W), BF16)] * 2,
        compiler_params=_params("parallel", "parallel", "parallel"),
        name=f"dil_attn_g{grp}",
    )(*([qkv] * 7))


MLA_HEADS_PER_STEP = 4


def _mla_attn_kernel(q_ref, k_ref, v_ref, o_ref, *, tq, unroll):
    lane = lax.broadcasted_iota(jnp.int32, (tq, 2 * MLA_V), 1)
    head = lambda h: slice(h * MLA_HEAD_PAD, (h + 1) * MLA_HEAD_PAD)
    units = [(u, h) for u in range(unroll) for h in range(MLA_HEADS_PER_STEP)]

    def trip(t, carry):
        rows = [pl.ds(pl.multiple_of((t * unroll + u) * tq, tq), tq) for u in range(unroll)]
        scores = lambda un: _dot_nt(q_ref[rows[un[0]], head(un[1])], k_ref[:, head(un[1])])
        outs = {}
        s_next = scores(units[0])
        for n, (u, h) in enumerate(units):
            s = s_next
            if n + 1 < len(units):
                s_next = scores(units[n + 1])
            m = jnp.max(s, axis=-1, keepdims=True)
            p = jnp.exp2(s - m).astype(BF16)
            ov = _dot(p, v_ref[:, head(h)])
            lcol = MLA_V if h % 2 == 0 else 0
            outs[h] = ov / ov[:, lcol:lcol + 1]
            if h % 2 == 1:
                pair = jnp.where(lane < MLA_V, outs[h - 1], outs[h])
                o_ref[rows[u], (h // 2) * 2 * MLA_V:(h // 2 + 1) * 2 * MLA_V] = pair.astype(BF16)
        return carry

    lax.fori_loop(0, q_ref.shape[0] // (tq * unroll), trip, 0)


def _mla_attn(qkv, B, S, tq, tiles):
    T = B * S
    bq = tq * tiles
    nq = S // bq
    nh = MLA_HEADS_PER_STEP
    ng = MLA_HEADS // nh
    kv_bytes = 2 * S * nh * MLA_HEAD_PAD * 2
    roomy = 2 * kv_bytes <= VMEM_LIMIT // 2
    mode = {} if roomy else dict(pipeline_mode=pl.Buffered(1))
    kv = lambda which: pl.BlockSpec((S, nh * MLA_HEAD_PAD), lambda b, p, i: (b, which * ng + p), **mode)
    return pl.pallas_call(
        functools.partial(_mla_attn_kernel, tq=tq, unroll=2 if roomy else 1),
        grid=(B, ng, nq),
        in_specs=[pl.BlockSpec((bq, nh * MLA_HEAD_PAD), lambda b, p, i: (b * nq + i, p)), kv(1), kv(2)],
        out_specs=pl.BlockSpec((bq, nh * MLA_V), lambda b, p, i: (b * nq + i, p)),
        out_shape=jax.ShapeDtypeStruct((T, MLA_HEADS * MLA_V), BF16),
        compiler_params=_params("parallel", "parallel", "parallel"),
        name="mla_attn",
    )(qkv, qkv, qkv)


def _pool_fill(up_ref, uc_ref, un_ref, ext, iseq, nt, tm):
    hl = POOL_HALO
    ext[0:hl] = jnp.where(iseq > 0, up_ref[...], 0.0)
    ext[hl:hl + tm] = uc_ref[...]
    ext[hl + tm:] = jnp.where(iseq < nt - 1, un_ref[...], 0.0)


def _pool_group(g, uc_ref, w_ref, sc_ref, cnt_ref, ext, tm):
    hl, w = POOL_HALO, POOL_WINDOWS[g]
    cols = slice(g * POOL_GW, (g + 1) * POOL_GW)
    acc = ext[hl - w // 2:hl - w // 2 + tm, cols]
    for off in range(-w // 2 + 1, w // 2):
        acc = acc + ext[hl + off:hl + off + tm, cols]
    mixed = acc / cnt_ref[:, g:g + 1] - uc_ref[:, cols]
    return _dot(mixed.astype(BF16), w_ref[g]) * sc_ref[:, cols]


def _dil_to_token_order(grp, o_ref, l_ref, o_nat, l_nat, tm):
    d = DIL_PAIRS[grp][1]
    for r in range(d):
        rows = pl.ds(r, tm // d, stride=d) if d > 1 else slice(None)
        l_nat[grp, rows, :] = l_ref[0, r]
        for hh in range(DIL_HEADS):
            sl = slice(hh * DIL_HEAD_DIM, (hh + 1) * DIL_HEAD_DIM)
            o_nat[grp, hh, rows, :] = o_ref[0, r, :, sl].astype(F32)


def _dil_combine_head(hh, o_nat, l_nat):
    ls = [l_nat[grp, :, hh:hh + 1] for grp in range(DIL_GROUPS)]
    mx = jnp.maximum(jnp.maximum(ls[0], ls[1]), ls[2])
    es = [jnp.exp2(l - mx) for l in ls]
    den = es[0] + es[1] + es[2]
    return sum((e / den) * o_nat[grp, hh] for grp, e in enumerate(es))


def _gate2(z_half):
    return jnp.tanh(z_half) + 1.0


def _merge_kernel(x_ref, g_ref, up_ref, uc_ref, un_ref, cnt_ref, b_ref, o0_ref, o1_ref, o2_ref, l0_ref, l1_ref, l2_ref,
                  pw_ref, ps_ref, wg_ref, wa_ref, wb_ref, wc_ref, wo_ref, x2_ref, ext, o_nat, l_nat, *, tm, nt):
    x = x_ref[...]
    h = _rms(x, g_ref[...]).astype(BF16)
    iseq = pl.program_id(0) % nt
    hw = D_MODEL // 2
    cols = lambda hf: slice(hf * hw, (hf + 1) * hw)
    logits = lambda j, hf: _dot(h, wg_ref[:, j * D_MODEL + hf * hw:j * D_MODEL + (hf + 1) * hw])
    pool = lambda g: _pool_group(g, uc_ref, pw_ref, ps_ref, cnt_ref, ext, tm)
    o_refs, l_refs = (o0_ref, o1_ref, o2_ref), (l0_ref, l1_ref, l2_ref)
    b = b_ref[...]

    l1a = logits(1, 0)
    _pool_fill(up_ref, uc_ref, un_ref, ext, iseq, nt, tm)
    a0 = pool(0)
    tba = _dot(b, wb_ref[:, cols(0)])
    a1 = pool(1)
    l1b = logits(1, 1)
    a2 = pool(2)
    tbb = _dot(b, wb_ref[:, cols(1)])
    l0a = logits(0, 0)
    a3 = pool(3)
    a = jnp.concatenate([a0, a1, a2, a3], axis=1).astype(BF16)
    l0b = logits(0, 1)
    _dil_to_token_order(0, o_refs[0], l_refs[0], o_nat, l_nat, tm)
    _dil_to_token_order(1, o_refs[1], l_refs[1], o_nat, l_nat, tm)
    taa = _dot(a, wa_ref[:, cols(0)])
    _dil_to_token_order(2, o_refs[2], l_refs[2], o_nat, l_nat, tm)
    l2a = logits(2, 0)
    c01 = [_dil_combine_head(hh, o_nat, l_nat) for hh in (0, 1)]
    tab = _dot(a, wa_ref[:, cols(1)])
    c23 = [_dil_combine_head(hh, o_nat, l_nat) for hh in (2, 3)]
    c = jnp.concatenate(c01 + c23, axis=1).astype(BF16)
    l2b = logits(2, 1)
    gba = _gate2(l1a) * tba + _gate2(l0a) * taa
    tca = _dot(c, wc_ref[:, cols(0)])
    gbb = _gate2(l1b) * tbb + _gate2(l0b) * tab
    tcb = _dot(c, wc_ref[:, cols(1)])
    ma = (gba + _gate2(l2a) * tca).astype(BF16)
    acc = x + _dot(ma, wo_ref[cols(0), :])
    mb = (gbb + _gate2(l2b) * tcb).astype(BF16)
    x2_ref[...] = acc + _dot(mb, wo_ref[cols(1), :])


def _merge(x, u, cnt, b, os_, ls_, p, S, tm):
    T = x.shape[0]
    nt = S // tm
    hb = tm // POOL_HALO
    nhb = T // POOL_HALO
    row = lambda i: (i, 0)
    cls = lambda i: (i // nt, 0, i % nt, 0)
    br = pl.BlockSpec((tm, BRANCH_W), row)
    halo = lambda f: pl.BlockSpec((POOL_HALO, POOL_WIDTH), f)
    ds_ = [d for _, d in DIL_PAIRS]
    weights = [p[n] for n in ("pool_w", "pool_scale", "w_gate", "w_a", "w_b", "w_c", "w_o")]
    return pl.pallas_call(
        functools.partial(_merge_kernel, tm=tm, nt=nt),
        grid=(T // tm,),
        in_specs=[pl.BlockSpec((tm, D_MODEL), row), _resident((1, D_MODEL)),
                  halo(lambda i: (jnp.maximum(i * hb - 1, 0), 0)), br,
                  halo(lambda i: (jnp.minimum((i + 1) * hb, nhb - 1), 0)),
                  pl.BlockSpec((tm, 128), lambda i: (i % nt, 0)), br]
        + [pl.BlockSpec((1, d, tm // d, DIL_GROUP_W), cls) for d in ds_]
        + [pl.BlockSpec((1, d, tm // d, 128), cls) for d in ds_]
        + [_resident(w.shape) for w in weights],
        out_specs=pl.BlockSpec((tm, D_MODEL), row),
        out_shape=jax.ShapeDtypeStruct((T, D_MODEL), F32),
        scratch_shapes=[pltpu.VMEM((tm + 2 * POOL_HALO, POOL_WIDTH), F32),
                        pltpu.VMEM((DIL_GROUPS, DIL_HEADS, tm, DIL_HEAD_DIM), F32),
                        pltpu.VMEM((DIL_GROUPS, tm, 128), F32)],
        compiler_params=_params("parallel"),
        name="merge",
    )(x, p["ln1"], u, u, u, cnt, b, *os_, *ls_, *weights)


def _ffn_kernel(x_ref, g_ref, w1_ref, w2_ref, gf_ref, y_ref, *, final):
    x = x_ref[...]
    h = _rms(x, g_ref[...]).astype(BF16)
    acc = x
    for c in range(D_FF // FF_CHUNK):
        sl = slice(c * FF_CHUNK, (c + 1) * FF_CHUNK)
        f = jnp.maximum(_dot(h, w1_ref[:, sl]), 0.0)
        acc = acc + _dot((f * f).astype(BF16), w2_ref[sl, :])
    if final:
        acc = _rms(acc, gf_ref[...])
    y_ref[...] = acc


def _ffn(x, g, w1, w2, gf, final, tm):
    T = x.shape[0]
    row = lambda i: (i, 0)
    return pl.pallas_call(
        functools.partial(_ffn_kernel, final=final),
        grid=(T // tm,),
        in_specs=[pl.BlockSpec((tm, D_MODEL), row), _resident((1, D_MODEL)), _resident(w1.shape),
                  _resident(w2.shape), _resident((1, D_MODEL))],
        out_specs=pl.BlockSpec((tm, D_MODEL), row),
        out_shape=jax.ShapeDtypeStruct((T, D_MODEL), F32),
        compiler_params=_params("parallel"),
        name="ffn",
    )(x, g, w1, w2, gf)


def _rope_tables(S):
    pos = jnp.arange(S).astype(F32)[:, None]

    def cs(half):
        inv = ROPE_THETA ** (-jnp.arange(half, dtype=F32) / half)
        ang = pos * inv[None, :]
        return jnp.cos(ang), jnp.sin(ang)

    c, s = cs(DIL_HEAD_DIM // 2)
    dil = [c, c, -s, s]
    c, s = cs(MLA_ROPE // 2)
    half = MLA_ROPE // 2
    z = lambda n: jnp.zeros((S, n), F32)
    tail = MLA_HEAD_PAD - MLA_QK
    mla = [jnp.ones((S, MLA_NOPE), F32), c, c, z(tail)] + [z(MLA_NOPE), -s, z(half + tail)] \
        + [z(MLA_NOPE + half), s, z(tail)]
    return jnp.concatenate(dil + mla, axis=1)


def _pool_counts(S):
    t = jnp.arange(S)[:, None]
    half = jnp.asarray([w // 2 for w in POOL_WINDOWS] + [1] * (128 - len(POOL_WINDOWS)))[None, :]
    return (jnp.clip(t + half, 0, S) - jnp.clip(t - half, 0, S)).astype(F32)


def _layer_weights(l, ln1, w_in, pool_w, pool_scale, q_norm, kv_norm, w_uq, w_uk, w_uv,
                   w_a, w_b, w_c, w_o, ln2, w_ff1, w_ff2):
    o = IN_OFFSETS
    wi = w_in[l]
    pad = MLA_HEAD_PAD - MLA_QK
    kr = wi[:, o[3]:o[4]]
    zl = jnp.zeros((D_MODEL, MLA_NOPE), F32)
    zr = jnp.zeros((D_MODEL, pad), F32)
    w1 = jnp.concatenate([wi[:, o[0]:o[3]], zl, kr, zr], axis=1)
    uq = w_uq[l].reshape(MLA_Q_RANK, MLA_HEADS, MLA_QK)
    zq = jnp.zeros((MLA_Q_RANK, MLA_HEADS, pad), F32)
    wq = jnp.concatenate([uq, zq], axis=2)
    uk = w_uk[l].reshape(MLA_KV_RANK, MLA_HEADS, MLA_NOPE)
    wk = jnp.concatenate([uk, jnp.zeros((MLA_KV_RANK, MLA_HEADS, MLA_HEAD_PAD - MLA_NOPE), F32)], axis=2)
    uv = w_uv[l].reshape(MLA_KV_RANK, MLA_HEADS // 2, 2, MLA_V)
    zv = jnp.zeros((MLA_KV_RANK, MLA_HEADS // 2, MLA_V), F32)
    wv = jnp.stack([jnp.concatenate([uv[:, :, 0], zv], axis=2), jnp.concatenate([zv, uv[:, :, 1]], axis=2)], axis=2)
    hp = MLA_HEADS * MLA_HEAD_PAD
    b16 = lambda a: a.astype(BF16)
    return dict(
        ln1=ln1[l][None], ln2=ln2[l][None],
        w_dil=b16(wi[:, o[4]:o[7]]), w_gate=b16(0.5 * wi[:, o[7]:o[8]]), w1=b16(w1),
        q_norm=q_norm[l][None], kv_norm=kv_norm[l][None],
        wq=b16(wq.reshape(MLA_Q_RANK, hp)),
        wk=b16(wk.reshape(MLA_KV_RANK, hp)), wv=b16(wv.reshape(MLA_KV_RANK, hp)),
        pool_w=b16(pool_w[l]), pool_scale=pool_scale[l][None],
        w_a=b16(0.5 * w_a[l]), w_b=b16(0.5 * w_b[l]), w_c=b16(0.5 * w_c[l]), w_o=b16(w_o[l]),
        w_ff1=b16(w_ff1[l]), w_ff2=b16(w_ff2[l]),
    )


def _encoder_layer(x, B, S, p, tabs, cnt, gf, final):
    tm = min(512, S)
    *dil_qkv, u, qkv = _in_proj(x, p, tabs, B, S, tm)
    dil = [_dil_attn(dil_qkv[grp], grp, 2048) for grp in range(DIL_GROUPS)]
    b = _mla_attn(qkv, B, S, min(256, S), 4)
    x2 = _merge(x, u, cnt, b, [o for o, _ in dil], [l for _, l in dil], p, S, tm)
    return _ffn(x2, p["ln2"], p["w_ff1"], p["w_ff2"], gf, final, tm)


def kernel(x_prompt, x_sample, ln1, w_in, pool_w, pool_scale, q_norm, kv_norm, w_uq, w_uk, w_uv,
           w_a, w_b, w_c, w_o, ln2, w_ff1, w_ff2, final_norm):
    depth = w_in.shape[0]
    layers = [_layer_weights(l, ln1, w_in, pool_w, pool_scale, q_norm, kv_norm, w_uq, w_uk, w_uv,
                             w_a, w_b, w_c, w_o, ln2, w_ff1, w_ff2) for l in range(depth)]
    gf = final_norm[None]
    outs = []
    for x in (x_prompt, x_sample):
        B, S, D = x.shape
        tabs = _rope_tables(S)
        cnt = _pool_counts(S)
        h = x.reshape(B * S, D)
        for l in range(depth):
            h = _encoder_layer(h, B, S, layers[l], tabs, cnt, gf, l == depth - 1)
        outs.append(h.reshape(B, S, D))
    return tuple(outs)
```

```python
import functools

import numpy as np
import jax
import jax.numpy as jnp
from jax import lax
from jax.experimental import pallas as pl
from jax.experimental.pallas import tpu as pltpu

F32 = jnp.float32
BF16 = jnp.bfloat16

D_MODEL = 1024
RMS_EPS = 1e-6
ROPE_THETA = 10000.0
NEG_INF = -1e30
LOG2_E = 1.4426950408889634

POOL_WIDTH = 512
POOL_WINDOWS = (2, 4, 8, 16)
POOL_GW = POOL_WIDTH // len(POOL_WINDOWS)
POOL_HALO = 8

MLA_HEADS = 8
MLA_NOPE = 64
MLA_ROPE = 32
MLA_V = 64
MLA_QK = MLA_NOPE + MLA_ROPE
MLA_Q_RANK = 384
MLA_KV_RANK = 256
MLA_HEAD_PAD = 128

DIL_PAIRS = ((128, 1), (512, 4), (2048, 16))
DIL_GROUPS = len(DIL_PAIRS)
DIL_HEADS = 4
DIL_HEAD_DIM = 128
DIL_GROUP_W = DIL_HEADS * DIL_HEAD_DIM
DIL_QKV = DIL_GROUPS * DIL_GROUP_W
DIL_HALF_W = 64
DIL_QBLK = 128
DIL_AHEAD = 3
DIL_SPLIT = 4

N_BRANCH = 3
BRANCH_W = 512
D_FF = 4 * D_MODEL
FF_CHUNK = 1024

IN_SPLITS = (POOL_WIDTH, MLA_Q_RANK, MLA_KV_RANK, MLA_ROPE, DIL_QKV, DIL_QKV, DIL_QKV, N_BRANCH * D_MODEL)
IN_OFFSETS = [0] + [int(o) for o in np.cumsum(IN_SPLITS)]

VMEM_LIMIT = 56 * 1024 * 1024


def _params(*sem):
    return pltpu.CompilerParams(dimension_semantics=sem, vmem_limit_bytes=VMEM_LIMIT)


def _resident(shape):
    nd = len(shape)
    return pl.BlockSpec(shape, lambda *_: (0,) * nd, pipeline_mode=pl.Buffered(1))


def _rms(xf, g):
    return xf * lax.rsqrt(jnp.mean(xf * xf, axis=-1, keepdims=True) + RMS_EPS) * g


def _dot(a, b):
    return jnp.dot(a, b, preferred_element_type=F32)


def _dot_nt(a, b):
    return lax.dot_general(a, b, (((1,), (1,)), ((), ())), preferred_element_type=F32)


def _mla_proj(h, w1_ref, qn_ref, kvn_ref, wq_ref, wk_ref, wv_ref, cos, sin_lo, sin_hi, u_ref, qkv_ref):
    t = _dot(h, w1_ref[...])
    o_cq = POOL_WIDTH
    o_ckv = o_cq + MLA_Q_RANK
    o_kr = o_ckv + MLA_KV_RANK
    hp = MLA_HEADS * MLA_HEAD_PAD
    half = MLA_ROPE // 2

    def rope(z):
        return (z * cos + pltpu.roll(z, MLA_HEAD_PAD - half, axis=1) * sin_lo
                + pltpu.roll(z, half, axis=1) * sin_hi)

    u_ref[...] = t[:, :o_cq]
    cqn = _rms(t[:, o_cq:o_ckv], qn_ref[...]).astype(BF16)
    cn = _rms(t[:, o_ckv:o_kr], kvn_ref[...]).astype(BF16)
    kr = rope(t[:, o_kr:])
    qf = _dot(cqn, wq_ref[...])
    kf = _dot(cn, wk_ref[...])
    scale = MLA_QK ** -0.5 * LOG2_E
    for hh in range(MLA_HEADS):
        sl = slice(hh * MLA_HEAD_PAD, (hh + 1) * MLA_HEAD_PAD)
        qkv_ref[:, sl] = (rope(qf[:, sl]) * scale).astype(BF16)
        qkv_ref[:, hp + sl.start:hp + sl.stop] = (kf[:, sl] + kr).astype(BF16)
    lane = lax.broadcasted_iota(jnp.int32, (1, hp), 1)
    ones_lane = jnp.where((lane // MLA_HEAD_PAD) % 2 == 0, MLA_V, 0)
    ones = (lane % MLA_HEAD_PAD == ones_lane).astype(F32)
    qkv_ref[:, 2 * hp:] = (_dot(cn, wv_ref[...]) + ones).astype(BF16)


def _dil_proj(h, w_ref, cos, sin, out_refs, stage, mid, tm):
    scale = DIL_HEAD_DIM ** -0.5 * LOG2_E
    n = 0
    for grp in reversed(range(DIL_GROUPS)):
        d = DIL_PAIRS[grp][1]
        out = out_refs[grp]
        for which in range(3):
            c = which * DIL_GROUPS + grp
            acc = _dot(h, w_ref[:, c * DIL_GROUP_W:(c + 1) * DIL_GROUP_W])
            for hh in range(DIL_HEADS):
                sl = slice(hh * DIL_HEAD_DIM, (hh + 1) * DIL_HEAD_DIM)
                dst = slice(which * DIL_GROUP_W + sl.start, which * DIL_GROUP_W + sl.stop)
                blk = acc[:, sl]
                if which < 2:
                    blk = blk * cos + pltpu.roll(blk, DIL_HEAD_DIM // 2, axis=1) * sin
                if which == 0:
                    blk = blk * scale
                if d == 1:
                    out[0, 0, :, dst] = blk.astype(BF16)
                else:
                    stage[n % 2, hh] = blk
            if d > 1:
                for hh in range(DIL_HEADS):
                    sl = slice(hh * DIL_HEAD_DIM, (hh + 1) * DIL_HEAD_DIM)
                    dst = slice(which * DIL_GROUP_W + sl.start, which * DIL_GROUP_W + sl.stop)
                    if d == DIL_SPLIT * DIL_SPLIT:
                        for lo in range(DIL_SPLIT):
                            mid[n % 2, hh, lo] = stage[n % 2, hh, pl.ds(lo, tm // DIL_SPLIT, stride=DIL_SPLIT), :]
                        for r in range(d):
                            hi, lo = divmod(r, DIL_SPLIT)
                            rows = pl.ds(hi, tm // d, stride=DIL_SPLIT)
                            out[0, r, :, dst] = mid[n % 2, hh, lo, rows, :].astype(BF16)
                    else:
                        for r in range(d):
                            out[0, r, :, dst] = stage[n % 2, hh, pl.ds(r, tm // d, stride=d), :].astype(BF16)
                n += 1


def _in_proj_kernel(x_ref, g_ref, wd_ref, tab_ref, w1_ref, qn_ref, kvn_ref, wq_ref, wk_ref, wv_ref, *refs, tm):
    dil_out, (u_ref, qkv_ref), (stage, mid) = refs[:DIL_GROUPS], refs[DIL_GROUPS:DIL_GROUPS + 2], refs[-2:]
    h = _rms(x_ref[...], g_ref[...]).astype(BF16)
    tab = lambda n: tab_ref[:, n * 128:(n + 1) * 128]
    _mla_proj(h, w1_ref, qn_ref, kvn_ref, wq_ref, wk_ref, wv_ref, tab(2), tab(3), tab(4), u_ref, qkv_ref)
    _dil_proj(h, wd_ref, tab(0), tab(1), dil_out, stage, mid, tm)


def _in_proj(x, p, tabs, B, S, tm):
    T = x.shape[0]
    nt = S // tm
    row = lambda i: (i, 0)
    cls = lambda i: (i // nt, 0, i % nt, 0)
    hp = MLA_HEADS * MLA_HEAD_PAD
    ds_ = [d for _, d in DIL_PAIRS]
    weights = [p[n] for n in ("w1", "q_norm", "kv_norm", "wq", "wk", "wv")]
    return pl.pallas_call(
        functools.partial(_in_proj_kernel, tm=tm),
        grid=(T // tm,),
        in_specs=[pl.BlockSpec((tm, D_MODEL), row), _resident((1, D_MODEL)), _resident(p["w_dil"].shape),
                  pl.BlockSpec((tm, tabs.shape[1]), lambda i: (i % nt, 0))]
        + [_resident(w.shape) for w in weights],
        out_specs=[pl.BlockSpec((1, d, tm // d, DIL_QKV), cls) for d in ds_]
        + [pl.BlockSpec((tm, POOL_WIDTH), row), pl.BlockSpec((tm, 3 * hp), row)],
        out_shape=[jax.ShapeDtypeStruct((B, d, S // d, DIL_QKV), BF16) for d in ds_]
        + [jax.ShapeDtypeStruct((T, POOL_WIDTH), F32), jax.ShapeDtypeStruct((T, 3 * hp), BF16)],
        scratch_shapes=[pltpu.VMEM((2, DIL_HEADS, tm, DIL_HEAD_DIM), F32),
                        pltpu.VMEM((2, DIL_HEADS, DIL_SPLIT, tm // DIL_SPLIT, DIL_HEAD_DIM), F32)],
        compiler_params=_params("parallel"),
        name="in_proj",
    )(x, p["ln1"], p["w_dil"], tabs, *weights)


def _dil_attn_kernel(q_ref, kp_ref, kc_ref, kn_ref, vp_ref, vc_ref, vn_ref, o_ref, st_ref, kwin, vwin,
                     *, tl, L, dr):
    i = pl.program_id(2)
    hw = DIL_HALF_W
    for rr in range(dr):
        kwin[rr, 0:hw] = kp_ref[0, rr]
        kwin[rr, hw:hw + tl] = kc_ref[0, rr]
        kwin[rr, hw + tl:] = kn_ref[0, rr]
        vwin[rr, 0:hw] = vp_ref[0, rr]
        vwin[rr, hw:hw + tl] = vc_ref[0, rr]
        vwin[rr, hw + tl:] = vn_ref[0, rr]
    nk = DIL_QBLK + 2 * hw
    row = lax.broadcasted_iota(jnp.int32, (DIL_QBLK, nk), 0)
    col = lax.broadcasted_iota(jnp.int32, (DIL_QBLK, nk), 1)
    band = (col >= row) & (col - row <= 2 * hw)
    lane = lax.broadcasted_iota(jnp.int32, (DIL_QBLK, 128), 1)
    head = lambda hh: slice(hh * DIL_HEAD_DIM, (hh + 1) * DIL_HEAD_DIM)
    units = [(j, rr, hh) for j in range(tl // DIL_QBLK) for rr in range(dr) for hh in range(DIL_HEADS)]

    def scores(u):
        j, rr, hh = u
        r0 = j * DIL_QBLK
        return _dot_nt(q_ref[0, rr, r0:r0 + DIL_QBLK, head(hh)], kwin[rr, r0:r0 + nk, head(hh)])

    ahead = [scores(u) for u in units[:DIL_AHEAD]]
    bias = stat = None
    for n, (j, rr, hh) in enumerate(units):
        r0 = j * DIL_QBLK
        s = ahead.pop(0)
        if n + DIL_AHEAD < len(units):
            ahead.append(scores(units[n + DIL_AHEAD]))
        if rr == 0 and hh == 0:
            kidx = i * tl + r0 - hw + col
            bias = jnp.where(band & (kidx >= 0) & (kidx < L), 0.0, NEG_INF)
        if hh == 0:
            stat = jnp.zeros((DIL_QBLK, 128), F32)
        s = s + bias
        m = jnp.max(s, axis=-1, keepdims=True)
        p = jnp.exp2(s - m)
        l = jnp.sum(p, axis=-1, keepdims=True)
        o_ref[0, rr, r0:r0 + DIL_QBLK, head(hh)] = _dot(p.astype(BF16), vwin[rr, r0:r0 + nk, head(hh)]).astype(BF16)
        stat = jnp.where(lane == hh, m, jnp.where(lane == DIL_HEADS + hh, l, stat))
        if hh == DIL_HEADS - 1:
            st_ref[0, rr, r0:r0 + DIL_QBLK, :] = stat


def _dil_attn(qkv, grp, rows):
    B, d, L, _ = qkv.shape
    tl = min(rows, L)
    dr = min(d, max(1, rows // tl))
    hb = tl // DIL_HALF_W
    nhb = L // DIL_HALF_W
    cur = lambda w: lambda b, r, i: (b, r, i, w)
    prev = lambda w: lambda b, r, i: (b, r, jnp.maximum(i * hb - 1, 0), w)
    nxt = lambda w: lambda b, r, i: (b, r, jnp.minimum((i + 1) * hb, nhb - 1), w)
    blk = lambda n, f: pl.BlockSpec((1, dr, n, DIL_GROUP_W), f)
    return pl.pallas_call(
        functools.partial(_dil_attn_kernel, tl=tl, L=L, dr=dr),
        grid=(B, d // dr, L // tl),
        in_specs=[blk(tl, cur(0)), blk(DIL_HALF_W, prev(1)), blk(tl, cur(1)), blk(DIL_HALF_W, nxt(1)),
                  blk(DIL_HALF_W, prev(2)), blk(tl, cur(2)), blk(DIL_HALF_W, nxt(2))],
        out_specs=[pl.BlockSpec((1, dr, tl, DIL_GROUP_W), cur(0)), pl.BlockSpec((1, dr, tl, 128), cur(0))],
        out_shape=[jax.ShapeDtypeStruct((B, d, L, DIL_GROUP_W), BF16),
                   jax.ShapeDtypeStruct((B, d, L, 128), F32)],
        scratch_shapes=[pltpu.VMEM((dr, tl + 2 * DIL_HALF_W, DIL_GROUP_W), BF16)] * 2,
        compiler_params=_params("parallel", "parallel", "parallel"),
        name=f"dil_attn_g{grp}",
    )(*([qkv] * 7))


MLA_HEADS_PER_STEP = 4


def _mla_attn_kernel(q_ref, k_ref, v_ref, o_ref, *, tq, unroll):
    lane = lax.broadcasted_iota(jnp.int32, (tq, 2 * MLA_V), 1)
    head = lambda h: slice(h * MLA_HEAD_PAD, (h + 1) * MLA_HEAD_PAD)
    units = [(u, h) for u in range(unroll) for h in range(MLA_HEADS_PER_STEP)]

    def trip(t, carry):
        rows = [pl.ds(pl.multiple_of((t * unroll + u) * tq, tq), tq) for u in range(unroll)]
        scores = lambda un: _dot_nt(q_ref[rows[un[0]], head(un[1])], k_ref[:, head(un[1])])
        outs = {}
        s_next = scores(units[0])
        for n, (u, h) in enumerate(units):
            s = s_next
            if n + 1 < len(units):
                s_next = scores(units[n + 1])
            m = jnp.max(s, axis=-1, keepdims=True)
            p = jnp.exp2(s - m).astype(BF16)
            ov = _dot(p, v_ref[:, head(h)])
            lcol = MLA_V if h % 2 == 0 else 0
            outs[h] = ov / ov[:, lcol:lcol + 1]
            if h % 2 == 1:
                pair = jnp.where(lane < MLA_V, outs[h - 1], outs[h])
                o_ref[rows[u], (h // 2) * 2 * MLA_V:(h // 2 + 1) * 2 * MLA_V] = pair.astype(BF16)
        return carry

    lax.fori_loop(0, q_ref.shape[0] // (tq * unroll), trip, 0)


def _mla_attn(qkv, B, S, tq, tiles):
    T = B * S
    bq = tq * tiles
    nq = S // bq
    nh = MLA_HEADS_PER_STEP
    ng = MLA_HEADS // nh
    kv_bytes = 2 * S * nh * MLA_HEAD_PAD * 2
    roomy = 2 * kv_bytes <= VMEM_LIMIT // 2
    mode = {} if roomy else dict(pipeline_mode=pl.Buffered(1))
    kv = lambda which: pl.BlockSpec((S, nh * MLA_HEAD_PAD), lambda b, p, i: (b, which * ng + p), **mode)
    return pl.pallas_call(
        functools.partial(_mla_attn_kernel, tq=tq, unroll=2 if roomy else 1),
        grid=(B, ng, nq),
        in_specs=[pl.BlockSpec((bq, nh * MLA_HEAD_PAD), lambda b, p, i: (b * nq + i, p)), kv(1), kv(2)],
        out_specs=pl.BlockSpec((bq, nh * MLA_V), lambda b, p, i: (b * nq + i, p)),
        out_shape=jax.ShapeDtypeStruct((T, MLA_HEADS * MLA_V), BF16),
        compiler_params=_params("parallel", "parallel", "parallel"),
        name="mla_attn",
    )(qkv, qkv, qkv)


def _pool_fill(up_ref, uc_ref, un_ref, ext, iseq, nt, tm):
    hl = POOL_HALO
    ext[0:hl] = jnp.where(iseq > 0, up_ref[...], 0.0)
    ext[hl:hl + tm] = uc_ref[...]
    ext[hl + tm:] = jnp.where(iseq < nt - 1, un_ref[...], 0.0)


def _pool_group(g, uc_ref, w_ref, sc_ref, cnt_ref, ext, tm):
    hl, w = POOL_HALO, POOL_WINDOWS[g]
    cols = slice(g * POOL_GW, (g + 1) * POOL_GW)
    acc = ext[hl - w // 2:hl - w // 2 + tm, cols]
    for off in range(-w // 2 + 1, w // 2):
        acc = acc + ext[hl + off:hl + off + tm, cols]
    mixed = acc / cnt_ref[:, g:g + 1] - uc_ref[:, cols]
    return _dot(mixed.astype(BF16), w_ref[g]) * sc_ref[:, cols]


def _dil_to_token_order(grp, o_ref, l_ref, o_nat, l_nat, tm):
    d = DIL_PAIRS[grp][1]
    for r in range(d):
        rows = pl.ds(r, tm // d, stride=d) if d > 1 else slice(None)
        l_nat[grp, rows, :] = l_ref[0, r]
        for hh in range(DIL_HEADS):
            sl = slice(hh * DIL_HEAD_DIM, (hh + 1) * DIL_HEAD_DIM)
            o_nat[grp, hh, rows, :] = o_ref[0, r, :, sl].astype(F32)


def _dil_combine_head(hh, o_nat, l_nat):
    ms = [l_nat[grp, :, hh:hh + 1] for grp in range(DIL_GROUPS)]
    ls = [l_nat[grp, :, DIL_HEADS + hh:DIL_HEADS + hh + 1] for grp in range(DIL_GROUPS)]
    mx = jnp.maximum(jnp.maximum(ms[0], ms[1]), ms[2])
    ws = [jnp.exp2(m - mx) for m in ms]
    den = ws[0] * ls[0] + ws[1] * ls[1] + ws[2] * ls[2]
    return sum((w / den) * o_nat[grp, hh] for grp, w in enumerate(ws))


def _gate2(z_half):
    return jnp.tanh(z_half) + 1.0


def _merge_kernel(x_ref, g_ref, up_ref, uc_ref, un_ref, cnt_ref, b_ref, o0_ref, o1_ref, o2_ref, l0_ref, l1_ref, l2_ref,
                  pw_ref, ps_ref, wg_ref, wa_ref, wb_ref, wc_ref, wo_ref, x2_ref, ext, o_nat, l_nat, *, tm, nt):
    x = x_ref[...]
    h = _rms(x, g_ref[...]).astype(BF16)
    iseq = pl.program_id(0) % nt
    hw = D_MODEL // 2
    cols = lambda hf: slice(hf * hw, (hf + 1) * hw)
    logits = lambda j, hf: _dot(h, wg_ref[:, j * D_MODEL + hf * hw:j * D_MODEL + (hf + 1) * hw])
    pool = lambda g: _pool_group(g, uc_ref, pw_ref, ps_ref, cnt_ref, ext, tm)
    o_refs, l_refs = (o0_ref, o1_ref, o2_ref), (l0_ref, l1_ref, l2_ref)
    b = b_ref[...]

    l1a = logits(1, 0)
    _pool_fill(up_ref, uc_ref, un_ref, ext, iseq, nt, tm)
    a0 = pool(0)
    tba = _dot(b, wb_ref[:, cols(0)])
    a1 = pool(1)
    l1b = logits(1, 1)
    a2 = pool(2)
    tbb = _dot(b, wb_ref[:, cols(1)])
    l0a = logits(0, 0)
    a3 = pool(3)
    a = jnp.concatenate([a0, a1, a2, a3], axis=1).astype(BF16)
    l0b = logits(0, 1)
    _dil_to_token_order(0, o_refs[0], l_refs[0], o_nat, l_nat, tm)
    _dil_to_token_order(1, o_refs[1], l_refs[1], o_nat, l_nat, tm)
    taa = _dot(a, wa_ref[:, cols(0)])
    _dil_to_token_order(2, o_refs[2], l_refs[2], o_nat, l_nat, tm)
    l2a = logits(2, 0)
    c01 = [_dil_combine_head(hh, o_nat, l_nat) for hh in (0, 1)]
    tab = _dot(a, wa_ref[:, cols(1)])
    c23 = [_dil_combine_head(hh, o_nat, l_nat) for hh in (2, 3)]
    c = jnp.concatenate(c01 + c23, axis=1).astype(BF16)
    l2b = logits(2, 1)
    gba = _gate2(l1a) * tba + _gate2(l0a) * taa
    tca = _dot(c, wc_ref[:, cols(0)])
    gbb = _gate2(l1b) * tbb + _gate2(l0b) * tab
    tcb = _dot(c, wc_ref[:, cols(1)])
    ma = (gba + _gate2(l2a) * tca).astype(BF16)
    acc = x + _dot(ma, wo_ref[cols(0), :])
    mb = (gbb + _gate2(l2b) * tcb).astype(BF16)
    x2_ref[...] = acc + _dot(mb, wo_ref[cols(1), :])


def _merge(x, u, cnt, b, os_, ls_, p, S, tm):
    T = x.shape[0]
    nt = S // tm
    hb = tm // POOL_HALO
    nhb = T // POOL_HALO
    row = lambda i: (i, 0)
    cls = lambda i: (i // nt, 0, i % nt, 0)
    br = pl.BlockSpec((tm, BRANCH_W), row)
    halo = lambda f: pl.BlockSpec((POOL_HALO, POOL_WIDTH), f)
    ds_ = [d for _, d in DIL_PAIRS]
    weights = [p[n] for n in ("pool_w", "pool_scale", "w_gate", "w_a", "w_b", "w_c", "w_o")]
    return pl.pallas_call(
        functools.partial(_merge_kernel, tm=tm, nt=nt),
        grid=(T // tm,),
        in_specs=[pl.BlockSpec((tm, D_MODEL), row), _resident((1, D_MODEL)),
                  halo(lambda i: (jnp.maximum(i * hb - 1, 0), 0)), br,
                  halo(lambda i: (jnp.minimum((i + 1) * hb, nhb - 1), 0)),
                  pl.BlockSpec((tm, 128), lambda i: (i % nt, 0)), br]
        + [pl.BlockSpec((1, d, tm // d, DIL_GROUP_W), cls) for d in ds_]
        + [pl.BlockSpec((1, d, tm // d, 128), cls) for d in ds_]
        + [_resident(w.shape) for w in weights],
        out_specs=pl.BlockSpec((tm, D_MODEL), row),
        out_shape=jax.ShapeDtypeStruct((T, D_MODEL), F32),
        scratch_shapes=[pltpu.VMEM((tm + 2 * POOL_HALO, POOL_WIDTH), F32),
                        pltpu.VMEM((DIL_GROUPS, DIL_HEADS, tm, DIL_HEAD_DIM), F32),
                        pltpu.VMEM((DIL_GROUPS, tm, 128), F32)],
        compiler_params=_params("parallel"),
        name="merge",
    )(x, p["ln1"], u, u, u, cnt, b, *os_, *ls_, *weights)


def _ffn_kernel(x_ref, g_ref, w1_ref, w2_ref, gf_ref, y_ref, *, final):
    x = x_ref[...]
    h = _rms(x, g_ref[...]).astype(BF16)
    acc = x
    for c in range(D_FF // FF_CHUNK):
        sl = slice(c * FF_CHUNK, (c + 1) * FF_CHUNK)
        f = jnp.maximum(_dot(h, w1_ref[:, sl]), 0.0)
        acc = acc + _dot((f * f).astype(BF16), w2_ref[sl, :])
    if final:
        acc = _rms(acc, gf_ref[...])
    y_ref[...] = acc


def _ffn(x, g, w1, w2, gf, final, tm):
    T = x.shape[0]
    row = lambda i: (i, 0)
    return pl.pallas_call(
        functools.partial(_ffn_kernel, final=final),
        grid=(T // tm,),
        in_specs=[pl.BlockSpec((tm, D_MODEL), row), _resident((1, D_MODEL)), _resident(w1.shape),
                  _resident(w2.shape), _resident((1, D_MODEL))],
        out_specs=pl.BlockSpec((tm, D_MODEL), row),
        out_shape=jax.ShapeDtypeStruct((T, D_MODEL), F32),
        compiler_params=_params("parallel"),
        name="ffn",
    )(x, g, w1, w2, gf)


def _rope_tables(S):
    pos = jnp.arange(S).astype(F32)[:, None]

    def cs(half):
        inv = ROPE_THETA ** (-jnp.arange(half, dtype=F32) / half)
        ang = pos * inv[None, :]
        return jnp.cos(ang), jnp.sin(ang)

    c, s = cs(DIL_HEAD_DIM // 2)
    dil = [c, c, -s, s]
    c, s = cs(MLA_ROPE // 2)
    half = MLA_ROPE // 2
    z = lambda n: jnp.zeros((S, n), F32)
    tail = MLA_HEAD_PAD - MLA_QK
    mla = [jnp.ones((S, MLA_NOPE), F32), c, c, z(tail)] + [z(MLA_NOPE), -s, z(half + tail)] \
        + [z(MLA_NOPE + half), s, z(tail)]
    return jnp.concatenate(dil + mla, axis=1)


def _pool_counts(S):
    t = jnp.arange(S)[:, None]
    half = jnp.asarray([w // 2 for w in POOL_WINDOWS] + [1] * (128 - len(POOL_WINDOWS)))[None, :]
    return (jnp.clip(t + half, 0, S) - jnp.clip(t - half, 0, S)).astype(F32)


def _layer_weights(l, ln1, w_in, pool_w, pool_scale, q_norm, kv_norm, w_uq, w_uk, w_uv,
                   w_a, w_b, w_c, w_o, ln2, w_ff1, w_ff2):
    o = IN_OFFSETS
    wi = w_in[l]
    pad = MLA_HEAD_PAD - MLA_QK
    kr = wi[:, o[3]:o[4]]
    zl = jnp.zeros((D_MODEL, MLA_NOPE), F32)
    zr = jnp.zeros((D_MODEL, pad), F32)
    w1 = jnp.concatenate([wi[:, o[0]:o[3]], zl, kr, zr], axis=1)
    uq = w_uq[l].reshape(MLA_Q_RANK, MLA_HEADS, MLA_QK)
    zq = jnp.zeros((MLA_Q_RANK, MLA_HEADS, pad), F32)
    wq = jnp.concatenate([uq, zq], axis=2)
    uk = w_uk[l].reshape(MLA_KV_RANK, MLA_HEADS, MLA_NOPE)
    wk = jnp.concatenate([uk, jnp.zeros((MLA_KV_RANK, MLA_HEADS, MLA_HEAD_PAD - MLA_NOPE), F32)], axis=2)
    uv = w_uv[l].reshape(MLA_KV_RANK, MLA_HEADS // 2, 2, MLA_V)
    zv = jnp.zeros((MLA_KV_RANK, MLA_HEADS // 2, MLA_V), F32)
    wv = jnp.stack([jnp.concatenate([uv[:, :, 0], zv], axis=2), jnp.concatenate([zv, uv[:, :, 1]], axis=2)], axis=2)
    hp = MLA_HEADS * MLA_HEAD_PAD
    b16 = lambda a: a.astype(BF16)
    return dict(
        ln1=ln1[l][None], ln2=ln2[l][None],
        w_dil=b16(wi[:, o[4]:o[7]]), w_gate=b16(0.5 * wi[:, o[7]:o[8]]), w1=b16(w1),
        q_norm=q_norm[l][None], kv_norm=kv_norm[l][None],
        wq=b16(wq.reshape(MLA_Q_RANK, hp)),
        wk=b16(wk.reshape(MLA_KV_RANK, hp)), wv=b16(wv.reshape(MLA_KV_RANK, hp)),
        pool_w=b16(pool_w[l]), pool_scale=pool_scale[l][None],
        w_a=b16(0.5 * w_a[l]), w_b=b16(0.5 * w_b[l]), w_c=b16(0.5 * w_c[l]), w_o=b16(w_o[l]),
        w_ff1=b16(w_ff1[l]), w_ff2=b16(w_ff2[l]),
    )


def _encoder_layer(x, B, S, p, tabs, cnt, gf, final):
    tm = min(512, S)
    *dil_qkv, u, qkv = _in_proj(x, p, tabs, B, S, tm)
    dil = [_dil_attn(dil_qkv[grp], grp, 2048) for grp in range(DIL_GROUPS)]
    b = _mla_attn(qkv, B, S, min(256, S), 4)
    x2 = _merge(x, u, cnt, b, [o for o, _ in dil], [l for _, l in dil], p, S, tm)
    return _ffn(x2, p["ln2"], p["w_ff1"], p["w_ff2"], gf, final, tm)


def kernel(x_prompt, x_sample, ln1, w_in, pool_w, pool_scale, q_norm, kv_norm, w_uq, w_uk, w_uv,
           w_a, w_b, w_c, w_o, ln2, w_ff1, w_ff2, final_norm):
    depth = w_in.shape[0]
    layers = [_layer_weights(l, ln1, w_in, pool_w, pool_scale, q_norm, kv_norm, w_uq, w_uk, w_uv,
                             w_a, w_b, w_c, w_o, ln2, w_ff1, w_ff2) for l in range(depth)]
    gf = final_norm[None]
    outs = []
    for x in (x_prompt, x_sample):
        B, S, D = x.shape
        tabs = _rope_tables(S)
        cnt = _pool_counts(S)
        h = x.reshape(B * S, D)
        for l in range(depth):
            h = _encoder_layer(h, B, S, layers[l], tabs, cnt, gf, l == depth - 1)
        outs.append(h.reshape(B, S, D))
    return tuple(outs)
```

```python
import functools

import numpy as np
import jax
import jax.numpy as jnp
from jax import lax
from jax.experimental import pallas as pl
from jax.experimental.pallas import tpu as pltpu

F32 = jnp.float32
BF16 = jnp.bfloat16

D_MODEL = 1024
RMS_EPS = 1e-6
ROPE_THETA = 10000.0
NEG_INF = -1e30
LOG2_E = 1.4426950408889634

POOL_WIDTH = 512
POOL_WINDOWS = (2, 4, 8, 16)
POOL_GW = POOL_WIDTH // len(POOL_WINDOWS)
POOL_HALO = 8

MLA_HEADS = 8
MLA_NOPE = 64
MLA_ROPE = 32
MLA_V = 64
MLA_QK = MLA_NOPE + MLA_ROPE
MLA_Q_RANK = 384
MLA_KV_RANK = 256
MLA_HEAD_PAD = 128

DIL_PAIRS = ((128, 1), (512, 4), (2048, 16))
DIL_GROUPS = len(DIL_PAIRS)
DIL_HEADS = 4
DIL_HEAD_DIM = 128
DIL_GROUP_W = DIL_HEADS * DIL_HEAD_DIM
DIL_QKV = DIL_GROUPS * DIL_GROUP_W
DIL_HALF_W = 64
DIL_QBLK = 128
DIL_AHEAD = 3
DIL_SPLIT = 4

N_BRANCH = 3
BRANCH_W = 512
D_FF = 4 * D_MODEL
FF_CHUNK = 1024

IN_SPLITS = (POOL_WIDTH, MLA_Q_RANK, MLA_KV_RANK, MLA_ROPE, DIL_QKV, DIL_QKV, DIL_QKV, N_BRANCH * D_MODEL)
IN_OFFSETS = [0] + [int(o) for o in np.cumsum(IN_SPLITS)]

VMEM_LIMIT = 56 * 1024 * 1024


def _params(*sem):
    return pltpu.CompilerParams(dimension_semantics=sem, vmem_limit_bytes=VMEM_LIMIT)


def _resident(shape):
    nd = len(shape)
    return pl.BlockSpec(shape, lambda *_: (0,) * nd, pipeline_mode=pl.Buffered(1))


def _rms(xf, g):
    return xf * lax.rsqrt(jnp.mean(xf * xf, axis=-1, keepdims=True) + RMS_EPS) * g


def _dot(a, b):
    return jnp.dot(a, b, preferred_element_type=F32)


def _dot_nt(a, b):
    return lax.dot_general(a, b, (((1,), (1,)), ((), ())), preferred_element_type=F32)


def _mla_proj(h, w1_ref, qn_ref, kvn_ref, wq_ref, wk_ref, wv_ref, cos, sin_lo, sin_hi, u_ref, qkv_ref):
    t = _dot(h, w1_ref[...])
    o_cq = POOL_WIDTH
    o_ckv = o_cq + MLA_Q_RANK
    o_kr = o_ckv + MLA_KV_RANK
    hp = MLA_HEADS * MLA_HEAD_PAD
    half = MLA_ROPE // 2

    def rope(z):
        return (z * cos + pltpu.roll(z, MLA_HEAD_PAD - half, axis=1) * sin_lo
                + pltpu.roll(z, half, axis=1) * sin_hi)

    u_ref[...] = t[:, :o_cq]
    cqn = _rms(t[:, o_cq:o_ckv], qn_ref[...]).astype(BF16)
    cn = _rms(t[:, o_ckv:o_kr], kvn_ref[...]).astype(BF16)
    kr = rope(t[:, o_kr:])
    qf = _dot(cqn, wq_ref[...])
    kf = _dot(cn, wk_ref[...])
    scale = MLA_QK ** -0.5 * LOG2_E
    for hh in range(MLA_HEADS):
        sl = slice(hh * MLA_HEAD_PAD, (hh + 1) * MLA_HEAD_PAD)
        qkv_ref[:, sl] = (rope(qf[:, sl]) * scale).astype(BF16)
        qkv_ref[:, hp + sl.start:hp + sl.stop] = (kf[:, sl] + kr).astype(BF16)
    lane = lax.broadcasted_iota(jnp.int32, (1, hp), 1)
    ones_lane = jnp.where((lane // MLA_HEAD_PAD) % 2 == 0, MLA_V, 0)
    ones = (lane % MLA_HEAD_PAD == ones_lane).astype(F32)
    qkv_ref[:, 2 * hp:] = (_dot(cn, wv_ref[...]) + ones).astype(BF16)


def _dil_proj(h, w_ref, cos, sin, out_refs, stage, mid, tm):
    scale = DIL_HEAD_DIM ** -0.5 * LOG2_E
    n = 0
    for grp in reversed(range(DIL_GROUPS)):
        d = DIL_PAIRS[grp][1]
        out = out_refs[grp]
        for which in range(3):
            c = which * DIL_GROUPS + grp
            acc = _dot(h, w_ref[:, c * DIL_GROUP_W:(c + 1) * DIL_GROUP_W])
            for hh in range(DIL_HEADS):
                sl = slice(hh * DIL_HEAD_DIM, (hh + 1) * DIL_HEAD_DIM)
                dst = slice(which * DIL_GROUP_W + sl.start, which * DIL_GROUP_W + sl.stop)
                blk = acc[:, sl]
                if which < 2:
                    blk = blk * cos + pltpu.roll(blk, DIL_HEAD_DIM // 2, axis=1) * sin
                if which == 0:
                    blk = blk * scale
                if d == 1:
                    out[0, 0, :, dst] = blk.astype(BF16)
                else:
                    stage[n % 2, hh] = blk
            if d > 1:
                for hh in range(DIL_HEADS):
                    sl = slice(hh * DIL_HEAD_DIM, (hh + 1) * DIL_HEAD_DIM)
                    dst = slice(which * DIL_GROUP_W + sl.start, which * DIL_GROUP_W + sl.stop)
                    if d == DIL_SPLIT * DIL_SPLIT:
                        for lo in range(DIL_SPLIT):
                            mid[n % 2, hh, lo] = stage[n % 2, hh, pl.ds(lo, tm // DIL_SPLIT, stride=DIL_SPLIT), :]
                        for r in range(d):
                            hi, lo = divmod(r, DIL_SPLIT)
                            rows = pl.ds(hi, tm // d, stride=DIL_SPLIT)
                            out[0, r, :, dst] = mid[n % 2, hh, lo, rows, :].astype(BF16)
                    else:
                        for r in range(d):
                            out[0, r, :, dst] = stage[n % 2, hh, pl.ds(r, tm // d, stride=d), :].astype(BF16)
                n += 1


def _in_proj_kernel(x_ref, g_ref, wd_ref, tab_ref, w1_ref, qn_ref, kvn_ref, wq_ref, wk_ref, wv_ref, *refs, tm):
    dil_out, (u_ref, qkv_ref), (stage, mid) = refs[:DIL_GROUPS], refs[DIL_GROUPS:DIL_GROUPS + 2], refs[-2:]
    h = _rms(x_ref[...], g_ref[...]).astype(BF16)
    tab = lambda n: tab_ref[:, n * 128:(n + 1) * 128]
    _mla_proj(h, w1_ref, qn_ref, kvn_ref, wq_ref, wk_ref, wv_ref, tab(2), tab(3), tab(4), u_ref, qkv_ref)
    _dil_proj(h, wd_ref, tab(0), tab(1), dil_out, stage, mid, tm)


def _in_proj(x, p, tabs, B, S, tm):
    T = x.shape[0]
    nt = S // tm
    row = lambda i: (i, 0)
    cls = lambda i: (i // nt, 0, i % nt, 0)
    hp = MLA_HEADS * MLA_HEAD_PAD
    ds_ = [d for _, d in DIL_PAIRS]
    weights = [p[n] for n in ("w1", "q_norm", "kv_norm", "wq", "wk", "wv")]
    return pl.pallas_call(
        functools.partial(_in_proj_kernel, tm=tm),
        grid=(T // tm,),
        in_specs=[pl.BlockSpec((tm, D_MODEL), row), _resident((1, D_MODEL)), _resident(p["w_dil"].shape),
                  pl.BlockSpec((tm, tabs.shape[1]), lambda i: (i % nt, 0))]
        + [_resident(w.shape) for w in weights],
        out_specs=[pl.BlockSpec((1, d, tm // d, DIL_QKV), cls) for d in ds_]
        + [pl.BlockSpec((tm, POOL_WIDTH), row), pl.BlockSpec((tm, 3 * hp), row)],
        out_shape=[jax.ShapeDtypeStruct((B, d, S // d, DIL_QKV), BF16) for d in ds_]
        + [jax.ShapeDtypeStruct((T, POOL_WIDTH), F32), jax.ShapeDtypeStruct((T, 3 * hp), BF16)],
        scratch_shapes=[pltpu.VMEM((2, DIL_HEADS, tm, DIL_HEAD_DIM), F32),
                        pltpu.VMEM((2, DIL_HEADS, DIL_SPLIT, tm // DIL_SPLIT, DIL_HEAD_DIM), F32)],
        compiler_params=_params("parallel"),
        name="in_proj",
    )(x, p["ln1"], p["w_dil"], tabs, *weights)


def _dil_attn_kernel(q_ref, kp_ref, kc_ref, kn_ref, vp_ref, vc_ref, vn_ref, o_ref, st_ref, kwin, vwin,
                     *, tl, L, dr):
    i = pl.program_id(2)
    hw = DIL_HALF_W
    for rr in range(dr):
        kwin[rr, 0:hw] = kp_ref[0, rr]
        kwin[rr, hw:hw + tl] = kc_ref[0, rr]
        kwin[rr, hw + tl:] = kn_ref[0, rr]
        vwin[rr, 0:hw] = vp_ref[0, rr]
        vwin[rr, hw:hw + tl] = vc_ref[0, rr]
        vwin[rr, hw + tl:] = vn_ref[0, rr]
    nk = DIL_QBLK + 2 * hw
    row = lax.broadcasted_iota(jnp.int32, (DIL_QBLK, nk), 0)
    col = lax.broadcasted_iota(jnp.int32, (DIL_QBLK, nk), 1)
    band = (col >= row) & (col - row <= 2 * hw)
    lane = lax.broadcasted_iota(jnp.int32, (DIL_QBLK, 128), 1)
    head = lambda hh: slice(hh * DIL_HEAD_DIM, (hh + 1) * DIL_HEAD_DIM)
    units = [(j, rr, hh) for j in range(tl // DIL_QBLK) for rr in range(dr) for hh in range(DIL_HEADS)]

    def scores(u):
        j, rr, hh = u
        r0 = j * DIL_QBLK
        return _dot_nt(q_ref[0, rr, r0:r0 + DIL_QBLK, head(hh)], kwin[rr, r0:r0 + nk, head(hh)])

    ahead = [scores(u) for u in units[:DIL_AHEAD]]
    bias = stat = None
    for n, (j, rr, hh) in enumerate(units):
        r0 = j * DIL_QBLK
        s = ahead.pop(0)
        if n + DIL_AHEAD < len(units):
            ahead.append(scores(units[n + DIL_AHEAD]))
        if rr == 0 and hh == 0:
            kidx = i * tl + r0 - hw + col
            bias = jnp.where(band & (kidx >= 0) & (kidx < L), 0.0, NEG_INF)
        if hh == 0:
            stat = jnp.zeros((DIL_QBLK, 128), F32)
        s = s + bias
        m = jnp.max(s, axis=-1, keepdims=True)
        p = jnp.exp2(s - m)
        l = jnp.sum(p, axis=-1, keepdims=True)
        o_ref[0, rr, r0:r0 + DIL_QBLK, head(hh)] = _dot(p.astype(BF16), vwin[rr, r0:r0 + nk, head(hh)]).astype(BF16)
        stat = jnp.where(lane == hh, m, jnp.where(lane == DIL_HEADS + hh, l, stat))
        if hh == DIL_HEADS - 1:
            st_ref[0, rr, r0:r0 + DIL_QBLK, :] = stat


def _dil_attn(qkv, grp, rows):
    B, d, L, _ = qkv.shape
    tl = min(rows, L)
    dr = min(d, max(1, rows // tl))
    hb = tl // DIL_HALF_W
    nhb = L // DIL_HALF_W
    cur = lambda w: lambda b, r, i: (b, r, i, w)
    prev = lambda w: lambda b, r, i: (b, r, jnp.maximum(i * hb - 1, 0), w)
    nxt = lambda w: lambda b, r, i: (b, r, jnp.minimum((i + 1) * hb, nhb - 1), w)
    blk = lambda n, f: pl.BlockSpec((1, dr, n, DIL_GROUP_W), f)
    return pl.pallas_call(
        functools.partial(_dil_attn_kernel, tl=tl, L=L, dr=dr),
        grid=(B, d // dr, L // tl),
        in_specs=[blk(tl, cur(0)), blk(DIL_HALF_W, prev(1)), blk(tl, cur(1)), blk(DIL_HALF_W, nxt(1)),
                  blk(DIL_HALF_W, prev(2)), blk(tl, cur(2)), blk(DIL_HALF_W, nxt(2))],
        out_specs=[pl.BlockSpec((1, dr, tl, DIL_GROUP_W), cur(0)), pl.BlockSpec((1, dr, tl, 128), cur(0))],
        out_shape=[jax.ShapeDtypeStruct((B, d, L, DIL_GROUP_W), BF16),
                   jax.ShapeDtypeStruct((B, d, L, 128), F32)],
        scratch_shapes=[pltpu.VMEM((dr, tl + 2 * DIL_HALF_W, DIL_GROUP_W), BF16)] * 2,
        compiler_params=_params("parallel", "parallel", "parallel"),
        name=f"dil_attn_g{grp}",
    )(*([qkv] * 7))


MLA_HEADS_PER_STEP = 4


def _mla_attn_kernel(q_ref, k_ref, v_ref, o_ref, *, tq, unroll):
    lane = lax.broadcasted_iota(jnp.int32, (tq, 2 * MLA_V), 1)
    head = lambda h: slice(h * MLA_HEAD_PAD, (h + 1) * MLA_HEAD_PAD)
    units = [(u, h) for u in range(unroll) for h in range(MLA_HEADS_PER_STEP)]

    def trip(t, carry):
        rows = [pl.ds(pl.multiple_of((t * unroll + u) * tq, tq), tq) for u in range(unroll)]
        scores = lambda un: _dot_nt(q_ref[rows[un[0]], head(un[1])], k_ref[:, head(un[1])])
        outs = {}
        s_next = scores(units[0])
        for n, (u, h) in enumerate(units):
            s = s_next
            if n + 1 < len(units):
                s_next = scores(units[n + 1])
            m = jnp.max(s, axis=-1, keepdims=True)
            p = jnp.exp2(s - m).astype(BF16)
            ov = _dot(p, v_ref[:, head(h)])
            lcol = MLA_V if h % 2 == 0 else 0
            outs[h] = ov / ov[:, lcol:lcol + 1]
            if h % 2 == 1:
                pair = jnp.where(lane < MLA_V, outs[h - 1], outs[h])
                o_ref[rows[u], (h // 2) * 2 * MLA_V:(h // 2 + 1) * 2 * MLA_V] = pair.astype(BF16)
        return carry

    lax.fori_loop(0, q_ref.shape[0] // (tq * unroll), trip, 0)


def _mla_attn(qkv, B, S, tq, tiles):
    T = B * S
    bq = tq * tiles
    nq = S // bq
    nh = MLA_HEADS_PER_STEP
    ng = MLA_HEADS // nh
    kv_bytes = 2 * S * nh * MLA_HEAD_PAD * 2
    roomy = 2 * kv_bytes <= VMEM_LIMIT // 2
    mode = {} if roomy else dict(pipeline_mode=pl.Buffered(1))
    kv = lambda which: pl.BlockSpec((S, nh * MLA_HEAD_PAD), lambda b, p, i: (b, which * ng + p), **mode)
    return pl.pallas_call(
        functools.partial(_mla_attn_kernel, tq=tq, unroll=2 if roomy else 1),
        grid=(B, ng, nq),
        in_specs=[pl.BlockSpec((bq, nh * MLA_HEAD_PAD), lambda b, p, i: (b * nq + i, p)), kv(1), kv(2)],
        out_specs=pl.BlockSpec((bq, nh * MLA_V), lambda b, p, i: (b * nq + i, p)),
        out_shape=jax.ShapeDtypeStruct((T, MLA_HEADS * MLA_V), BF16),
        compiler_params=_params("parallel", "parallel", "parallel"),
        name="mla_attn",
    )(qkv, qkv, qkv)


def _pool_fill(up_ref, uc_ref, un_ref, ext, iseq, nt, tm):
    hl = POOL_HALO
    ext[0:hl] = jnp.where(iseq > 0, up_ref[...], 0.0)
    ext[hl:hl + tm] = uc_ref[...]
    ext[hl + tm:] = jnp.where(iseq < nt - 1, un_ref[...], 0.0)


def _pool_group(g, uc_ref, w_ref, sc_ref, cnt_ref, ext, tm):
    hl, w = POOL_HALO, POOL_WINDOWS[g]
    cols = slice(g * POOL_GW, (g + 1) * POOL_GW)
    acc = ext[hl - w // 2:hl - w // 2 + tm, cols]
    for off in range(-w // 2 + 1, w // 2):
        acc = acc + ext[hl + off:hl + off + tm, cols]
    mixed = acc / cnt_ref[:, g:g + 1] - uc_ref[:, cols]
    return _dot(mixed.astype(BF16), w_ref[g]) * sc_ref[:, cols]


def _dil_to_token_order(grp, o_ref, l_ref, o_nat, l_nat, tm):
    d = DIL_PAIRS[grp][1]
    for r in range(d):
        rows = pl.ds(r, tm // d, stride=d) if d > 1 else slice(None)
        l_nat[grp, rows, :] = l_ref[0, r]
        for hh in range(DIL_HEADS):
            sl = slice(hh * DIL_HEAD_DIM, (hh + 1) * DIL_HEAD_DIM)
            o_nat[grp, hh, rows, :] = o_ref[0, r, :, sl].astype(F32)


def _dil_combine_head(hh, o_nat, l_nat):
    ms = [l_nat[grp, :, hh:hh + 1] for grp in range(DIL_GROUPS)]
    ls = [l_nat[grp, :, DIL_HEADS + hh:DIL_HEADS + hh + 1] for grp in range(DIL_GROUPS)]
    mx = jnp.maximum(jnp.maximum(ms[0], ms[1]), ms[2])
    ws = [jnp.exp2(m - mx) for m in ms]
    den = ws[0] * ls[0] + ws[1] * ls[1] + ws[2] * ls[2]
    return sum((w / den) * o_nat[grp, hh] for grp, w in enumerate(ws))


def _gate2(z_half):
    return jnp.tanh(z_half) + 1.0


def _merge_kernel(x_ref, g_ref, up_ref, uc_ref, un_ref, cnt_ref, b_ref, o0_ref, o1_ref, o2_ref, l0_ref, l1_ref, l2_ref,
                  pw_ref, ps_ref, wg_ref, wa_ref, wb_ref, wc_ref, wo_ref, x2_ref, ext, o_nat, l_nat, *, tm, nt):
    x = x_ref[...]
    h = _rms(x, g_ref[...]).astype(BF16)
    iseq = pl.program_id(0) % nt
    hw = D_MODEL // 2
    cols = lambda hf: slice(hf * hw, (hf + 1) * hw)
    logits = lambda j, hf: _dot(h, wg_ref[:, j * D_MODEL + hf * hw:j * D_MODEL + (hf + 1) * hw])
    pool = lambda g: _pool_group(g, uc_ref, pw_ref, ps_ref, cnt_ref, ext, tm)
    o_refs, l_refs = (o0_ref, o1_ref, o2_ref), (l0_ref, l1_ref, l2_ref)
    b = b_ref[...]

    l1a = logits(1, 0)
    _pool_fill(up_ref, uc_ref, un_ref, ext, iseq, nt, tm)
    a0 = pool(0)
    tba = _dot(b, wb_ref[:, cols(0)])
    a1 = pool(1)
    l1b = logits(1, 1)
    a2 = pool(2)
    tbb = _dot(b, wb_ref[:, cols(1)])
    l0a = logits(0, 0)
    a3 = pool(3)
    a = jnp.concatenate([a0, a1, a2, a3], axis=1).astype(BF16)
    l0b = logits(0, 1)
    _dil_to_token_order(0, o_refs[0], l_refs[0], o_nat, l_nat, tm)
    _dil_to_token_order(1, o_refs[1], l_refs[1], o_nat, l_nat, tm)
    taa = _dot(a, wa_ref[:, cols(0)])
    _dil_to_token_order(2, o_refs[2], l_refs[2], o_nat, l_nat, tm)
    l2a = logits(2, 0)
    c01 = [_dil_combine_head(hh, o_nat, l_nat) for hh in (0, 1)]
    tab = _dot(a, wa_ref[:, cols(1)])
    c23 = [_dil_combine_head(hh, o_nat, l_nat) for hh in (2, 3)]
    c = jnp.concatenate(c01 + c23, axis=1).astype(BF16)
    l2b = logits(2, 1)
    gba = _gate2(l1a) * tba + _gate2(l0a) * taa
    tca = _dot(c, wc_ref[:, cols(0)])
    gbb = _gate2(l1b) * tbb + _gate2(l0b) * tab
    tcb = _dot(c, wc_ref[:, cols(1)])
    ma = (gba + _gate2(l2a) * tca).astype(BF16)
    acc = x + _dot(ma, wo_ref[cols(0), :])
    mb = (gbb + _gate2(l2b) * tcb).astype(BF16)
    x2_ref[...] = acc + _dot(mb, wo_ref[cols(1), :])


def _merge(x, u, cnt, b, os_, ls_, p, S, tm):
    T = x.shape[0]
    nt = S // tm
    hb = tm // POOL_HALO
    nhb = T // POOL_HALO
    row = lambda i: (i, 0)
    cls = lambda i: (i // nt, 0, i % nt, 0)
    br = pl.BlockSpec((tm, BRANCH_W), row)
    halo = lambda f: pl.BlockSpec((POOL_HALO, POOL_WIDTH), f)
    ds_ = [d for _, d in DIL_PAIRS]
    weights = [p[n] for n in ("pool_w", "pool_scale", "w_gate", "w_a", "w_b", "w_c", "w_o")]
    return pl.pallas_call(
        functools.partial(_merge_kernel, tm=tm, nt=nt),
        grid=(T // tm,),
        in_specs=[pl.BlockSpec((tm, D_MODEL), row), _resident((1, D_MODEL)),
                  halo(lambda i: (jnp.maximum(i * hb - 1, 0), 0)), br,
                  halo(lambda i: (jnp.minimum((i + 1) * hb, nhb - 1), 0)),
                  pl.BlockSpec((tm, 128), lambda i: (i % nt, 0)), br]
        + [pl.BlockSpec((1, d, tm // d, DIL_GROUP_W), cls) for d in ds_]
        + [pl.BlockSpec((1, d, tm // d, 128), cls) for d in ds_]
        + [_resident(w.shape) for w in weights],
        out_specs=pl.BlockSpec((tm, D_MODEL), row),
        out_shape=jax.ShapeDtypeStruct((T, D_MODEL), F32),
        scratch_shapes=[pltpu.VMEM((tm + 2 * POOL_HALO, POOL_WIDTH), F32),
                        pltpu.VMEM((DIL_GROUPS, DIL_HEADS, tm, DIL_HEAD_DIM), F32),
                        pltpu.VMEM((DIL_GROUPS, tm, 128), F32)],
        compiler_params=_params("parallel"),
        name="merge",
    )(x, p["ln1"], u, u, u, cnt, b, *os_, *ls_, *weights)


def _ffn_kernel(x_ref, g_ref, w1_ref, w2_ref, gf_ref, y_ref, *, final):
    x = x_ref[...]
    h = _rms(x, g_ref[...]).astype(BF16)
    acc = x
    for c in range(D_FF // FF_CHUNK):
        sl = slice(c * FF_CHUNK, (c + 1) * FF_CHUNK)
        f = jnp.maximum(_dot(h, w1_ref[:, sl]), 0.0)
        acc = acc + _dot((f * f).astype(BF16), w2_ref[sl, :])
    if final:
        acc = _rms(acc, gf_ref[...])
    y_ref[...] = acc


def _ffn(x, g, w1, w2, gf, final, tm):
    T = x.shape[0]
    row = lambda i: (i, 0)
    return pl.pallas_call(
        functools.partial(_ffn_kernel, final=final),
        grid=(T // tm,),
        in_specs=[pl.BlockSpec((tm, D_MODEL), row), _resident((1, D_MODEL)), _resident(w1.shape),
                  _resident(w2.shape), _resident((1, D_MODEL))],
        out_specs=pl.BlockSpec((tm, D_MODEL), row),
        out_shape=jax.ShapeDtypeStruct((T, D_MODEL), F32),
        compiler_params=_params("parallel"),
        name="ffn",
    )(x, g, w1, w2, gf)


def _rope_tables(S):
    pos = jnp.arange(S).astype(F32)[:, None]

    def cs(half):
        inv = ROPE_THETA ** (-jnp.arange(half, dtype=F32) / half)
        ang = pos * inv[None, :]
        return jnp.cos(ang), jnp.sin(ang)

    c, s = cs(DIL_HEAD_DIM // 2)
    dil = [c, c, -s, s]
    c, s = cs(MLA_ROPE // 2)
    half = MLA_ROPE // 2
    z = lambda n: jnp.zeros((S, n), F32)
    tail = MLA_HEAD_PAD - MLA_QK
    mla = [jnp.ones((S, MLA_NOPE), F32), c, c, z(tail)] + [z(MLA_NOPE), -s, z(half + tail)] \
        + [z(MLA_NOPE + half), s, z(tail)]
    return jnp.concatenate(dil + mla, axis=1)


def _pool_counts(S):
    t = jnp.arange(S)[:, None]
    half = jnp.asarray([w // 2 for w in POOL_WINDOWS] + [1] * (128 - len(POOL_WINDOWS)))[None, :]
    return (jnp.clip(t + half, 0, S) - jnp.clip(t - half, 0, S)).astype(F32)


def _layer_weights(l, ln1, w_in, pool_w, pool_scale, q_norm, kv_norm, w_uq, w_uk, w_uv,
                   w_a, w_b, w_c, w_o, ln2, w_ff1, w_ff2):
    o = IN_OFFSETS
    wi = w_in[l]
    pad = MLA_HEAD_PAD - MLA_QK
    kr = wi[:, o[3]:o[4]]
    zl = jnp.zeros((D_MODEL, MLA_NOPE), F32)
    zr = jnp.zeros((D_MODEL, pad), F32)
    w1 = jnp.concatenate([wi[:, o[0]:o[3]], zl, kr, zr], axis=1)
    uq = w_uq[l].reshape(MLA_Q_RANK, MLA_HEADS, MLA_QK)
    zq = jnp.zeros((MLA_Q_RANK, MLA_HEADS, pad), F32)
    wq = jnp.concatenate([uq, zq], axis=2)
    uk = w_uk[l].reshape(MLA_KV_RANK, MLA_HEADS, MLA_NOPE)
    wk = jnp.concatenate([uk, jnp.zeros((MLA_KV_RANK, MLA_HEADS, MLA_HEAD_PAD - MLA_NOPE), F32)], axis=2)
    uv = w_uv[l].reshape(MLA_KV_RANK, MLA_HEADS // 2, 2, MLA_V)
    zv = jnp.zeros((MLA_KV_RANK, MLA_HEADS // 2, MLA_V), F32)
    wv = jnp.stack([jnp.concatenate([uv[:, :, 0], zv], axis=2), jnp.concatenate([zv, uv[:, :, 1]], axis=2)], axis=2)
    hp = MLA_HEADS * MLA_HEAD_PAD
    b16 = lambda a: a.astype(BF16)
    return dict(
        ln1=ln1[l][None], ln2=ln2[l][None],
        w_dil=b16(wi[:, o[4]:o[7]]), w_gate=b16(0.5 * wi[:, o[7]:o[8]]), w1=b16(w1),
        q_norm=q_norm[l][None], kv_norm=kv_norm[l][None],
        wq=b16(wq.reshape(MLA_Q_RANK, hp)),
        wk=b16(wk.reshape(MLA_KV_RANK, hp)), wv=b16(wv.reshape(MLA_KV_RANK, hp)),
        pool_w=b16(pool_w[l]), pool_scale=pool_scale[l][None],
        w_a=b16(0.5 * w_a[l]), w_b=b16(0.5 * w_b[l]), w_c=b16(0.5 * w_c[l]), w_o=b16(w_o[l]),
        w_ff1=b16(w_ff1[l]), w_ff2=b16(w_ff2[l]),
    )


def _encoder_layer(x, B, S, p, tabs, cnt, gf, final):
    tm = min(512, S)
    *dil_qkv, u, qkv = _in_proj(x, p, tabs, B, S, tm)
    dil = [_dil_attn(dil_qkv[grp], grp, 2048) for grp in range(DIL_GROUPS)]
    b = _mla_attn(qkv, B, S, min(256, S), 8)
    x2 = _merge(x, u, cnt, b, [o for o, _ in dil], [l for _, l in dil], p, S, tm)
    return _ffn(x2, p["ln2"], p["w_ff1"], p["w_ff2"], gf, final, tm)


def kernel(x_prompt, x_sample, ln1, w_in, pool_w, pool_scale, q_norm, kv_norm, w_uq, w_uk, w_uv,
           w_a, w_b, w_c, w_o, ln2, w_ff1, w_ff2, final_norm):
    depth = w_in.shape[0]
    layers = [_layer_weights(l, ln1, w_in, pool_w, pool_scale, q_norm, kv_norm, w_uq, w_uk, w_uv,
                             w_a, w_b, w_c, w_o, ln2, w_ff1, w_ff2) for l in range(depth)]
    gf = final_norm[None]
    outs = []
    for x in (x_prompt, x_sample):
        B, S, D = x.shape
        tabs = _rope_tables(S)
        cnt = _pool_counts(S)
        h = x.reshape(B * S, D)
        for l in range(depth):
            h = _encoder_layer(h, B, S, layers[l], tabs, cnt, gf, l == depth - 1)
        outs.append(h.reshape(B, S, D))
    return tuple(outs)
```
